```python
import math
import jax, jax.numpy as jnp
from jax import lax
import numpy as np

D_MODEL = 1024
BATCH = 8
SEQ = 8192
DEPTH = 1
DEC_BATCH = 2
DEC_SEQ = 16384
PAST_LEN = 128

S5_WIDTH = D_MODEL
S5_GROUP_SIZE = 16
S5_GROUPS = S5_WIDTH // S5_GROUP_SIZE
S5_STATE = 64
S5_CHUNK = 128
LRU_WIDTH = D_MODEL
LRU_HEADS = 8
LRU_BLOCK = LRU_WIDTH // LRU_HEADS
CONV_WIDTH = 4
CONV_LEFT = 2
RG_C = 8.0
N_DIR = 2
EPS = 1e-6
OFF_S5_U = 0
OFF_S5_Z = OFF_S5_U + S5_WIDTH
OFF_LRU_U = OFF_S5_Z + S5_WIDTH
OFF_LRU_Z = OFF_LRU_U + LRU_WIDTH
OFF_G_S5 = OFF_LRU_Z + LRU_WIDTH
OFF_G_LRU = OFF_G_S5 + D_MODEL
PROJ_WIDTH = OFF_G_LRU + D_MODEL

kernel_name = "hybrid_s5_rglru_gated_encoder"


def _rmsnorm(x, g):
    x32 = x.astype(jnp.float32)
    y = x32 * lax.rsqrt(jnp.mean(x32 * x32, axis=-1, keepdims=True) + EPS)
    return (y * g.astype(jnp.float32)).astype(x.dtype)


def _combine(left, right):
    a1, b1 = left
    a2, b2 = right
    return a1 * a2, a2 * b1 + b2


def _s5_direction(u, a_re, a_im, log_dt, b_re, b_im, c_re, c_im, reverse):
    bt, seq = u.shape[0], u.shape[1]
    f32 = jnp.float32
    lam = lax.complex(a_re.astype(f32), a_im.astype(f32))
    dt = jnp.exp(log_dt.astype(f32))[:, None]
    a_bar = jnp.exp(lam * dt)
    b_mat = lax.complex(b_re.astype(f32), b_im.astype(f32))
    b_bar = ((a_bar - 1.0) / lam)[..., None] * b_mat
    c_mat = lax.complex(c_re.astype(f32), c_im.astype(f32))
    b_bar_re = jnp.real(b_bar)
    b_bar_im = jnp.imag(b_bar)
    n_chunks = seq // S5_CHUNK
    u_blocks = u.reshape(bt, n_chunks, S5_CHUNK, S5_GROUPS, S5_GROUP_SIZE).transpose(1, 0, 2, 3, 4)
    a_blk = jnp.broadcast_to(a_bar, (bt, S5_CHUNK, S5_GROUPS, S5_STATE))

    def step(h_carry, u_blk):
        bu = lax.complex(jnp.einsum('btgc,gpc->btgp', u_blk, b_bar_re),
                         jnp.einsum('btgc,gpc->btgp', u_blk, b_bar_im))
        a_cum, h_loc = lax.associative_scan(_combine, (a_blk, bu), axis=1, reverse=reverse)
        h_all = a_cum * h_carry[:, None] + h_loc
        y = jnp.real(jnp.einsum('btgp,gcp->btgc', h_all, c_mat))
        h_next = h_all[:, 0] if reverse else h_all[:, -1]
        return h_next, y

    h0 = jnp.zeros((bt, S5_GROUPS, S5_STATE), jnp.complex64)
    _, ys = lax.scan(step, h0, u_blocks, reverse=reverse)
    return ys.transpose(1, 0, 2, 3, 4).reshape(bt, seq, S5_GROUPS, S5_GROUP_SIZE)


def _conv_centred(x, w, b):
    seq = x.shape[1]
    xp = jnp.pad(x, ((0, 0), (CONV_LEFT, CONV_WIDTH - 1 - CONV_LEFT), (0, 0)))
    out = b
    for k in range(CONV_WIDTH):
        out = out + xp[:, k:k + seq] * w[k]
    return out


def _rglru_direction(xc, lam, wa, ba, wx, bx, reverse):
    bt, seq, width = xc.shape
    f32 = jnp.float32
    xb = xc.reshape(bt, seq, LRU_HEADS, LRU_BLOCK)
    r = jax.nn.sigmoid(jnp.einsum('blhi,hij->blhj', xb, wa.astype(f32)).reshape(bt, seq, width) + ba.astype(f32))
    i = jax.nn.sigmoid(jnp.einsum('blhi,hij->blhj', xb, wx.astype(f32)).reshape(bt, seq, width) + bx.astype(f32))
    log_a = -RG_C * r * jax.nn.softplus(-lam.astype(f32))
    a = jnp.exp(log_a)
    b = jnp.sqrt(-jnp.expm1(2.0 * log_a)) * (i * xc)
    _, h = lax.associative_scan(_combine, (a, b), axis=1, reverse=reverse)
    return h


def _layer(x, norm_g, w_in, s5_a_re, s5_a_im, s5_log_dt, s5_b_re, s5_b_im, s5_c_re, s5_c_im,
           s5_d, s5_glu_w, s5_glu_b, s5_proj, lru_conv_w, lru_conv_b, lru_lambda, lru_wa, lru_ba,
           lru_wx, lru_bx, lru_proj, w_out):
    bt, seq, _ = x.shape
    f32 = jnp.float32
    h = _rmsnorm(x, norm_g)
    p = h @ w_in
    u_s5 = p[..., OFF_S5_U:OFF_S5_Z]
    z_s5 = p[..., OFF_S5_Z:OFF_LRU_U]
    u_lru = p[..., OFF_LRU_U:OFF_LRU_Z]
    z_lru = p[..., OFF_LRU_Z:OFF_G_S5]
    g_s5 = p[..., OFF_G_S5:OFF_G_LRU]
    g_lru = p[..., OFF_G_LRU:PROJ_WIDTH]

    u = u_s5.astype(f32).reshape(bt, seq, S5_GROUPS, S5_GROUP_SIZE)
    y = (_s5_direction(u, s5_a_re[0], s5_a_im[0], s5_log_dt[0], s5_b_re[0], s5_b_im[0],
                       s5_c_re[0], s5_c_im[0], False)
         + _s5_direction(u, s5_a_re[1], s5_a_im[1], s5_log_dt[1], s5_b_re[1], s5_b_im[1],
                         s5_c_re[1], s5_c_im[1], True)
         + s5_d.astype(f32).reshape(S5_GROUPS, S5_GROUP_SIZE) * u)
    y = jax.nn.gelu(y.reshape(bt, seq, S5_WIDTH)).astype(x.dtype)
    y = y * jax.nn.sigmoid(y @ s5_glu_w + s5_glu_b)
    y_s5 = (y * jax.nn.silu(z_s5)) @ s5_proj

    xc = _conv_centred(u_lru.astype(f32), lru_conv_w.astype(f32), lru_conv_b.astype(f32))
    hl = (_rglru_direction(xc, lru_lambda[0], lru_wa[0], lru_ba[0], lru_wx[0], lru_bx[0], False)
          + _rglru_direction(xc, lru_lambda[1], lru_wa[1], lru_ba[1], lru_wx[1], lru_bx[1], True))
    y_lru = (hl.astype(x.dtype) * jax.nn.silu(z_lru)) @ lru_proj

    merged = jax.nn.sigmoid(g_s5) * y_s5 + jax.nn.sigmoid(g_lru) * y_lru
    return x + merged @ w_out


def _trunk(x, norm_g, w_in, s5_a_re, s5_a_im, s5_log_dt, s5_b_re, s5_b_im, s5_c_re, s5_c_im,
           s5_d, s5_glu_w, s5_glu_b, s5_proj, lru_conv_w, lru_conv_b, lru_lambda, lru_wa, lru_ba,
           lru_wx, lru_bx, lru_proj, w_out, norm_f_g):
    for l in range(DEPTH):
        x = _layer(x, norm_g[l], w_in[l], s5_a_re[l], s5_a_im[l], s5_log_dt[l], s5_b_re[l], s5_b_im[l],
                   s5_c_re[l], s5_c_im[l], s5_d[l], s5_glu_w[l], s5_glu_b[l], s5_proj[l],
                   lru_conv_w[l], lru_conv_b[l], lru_lambda[l], lru_wa[l], lru_ba[l],
                   lru_wx[l], lru_bx[l], lru_proj[l], w_out[l])
    return _rmsnorm(x, norm_f_g)


def setup_inputs(seed: int = 0) -> dict:
    key = jax.random.key(seed)
    ks = jax.random.split(key, 32)
    f32 = jnp.float32

    def nrm(k, shape, scale):
        return scale * jax.random.normal(k, shape, f32)

    G, P, GS = S5_GROUPS, S5_STATE, S5_GROUP_SIZE
    x_prompt = nrm(ks[0], (BATCH, SEQ, D_MODEL), 1.0)
    x_sample = nrm(ks[1], (DEC_BATCH, DEC_SEQ, D_MODEL), 1.0)
    norm_g = 1.0 + nrm(ks[2], (DEPTH, D_MODEL), 0.02)
    w_in = nrm(ks[3], (DEPTH, D_MODEL, PROJ_WIDTH), D_MODEL ** -0.5)
    n_idx = jnp.arange(P, dtype=f32)
    s5_a_re = -0.5 + nrm(ks[4], (DEPTH, N_DIR, G, P), 0.01)
    s5_a_im = math.pi * n_idx + nrm(ks[5], (DEPTH, N_DIR, G, P), 0.01)
    s5_log_dt = jax.random.uniform(ks[6], (DEPTH, N_DIR, G), f32, math.log(1e-3), math.log(1e-1))
    s5_b_re = nrm(ks[7], (DEPTH, N_DIR, G, P, GS), (2.0 * GS) ** -0.5)
    s5_b_im = nrm(ks[8], (DEPTH, N_DIR, G, P, GS), (2.0 * GS) ** -0.5)
    s5_c_re = nrm(ks[9], (DEPTH, N_DIR, G, GS, P), (2.0 * P) ** -0.5)
    s5_c_im = nrm(ks[10], (DEPTH, N_DIR, G, GS, P), (2.0 * P) ** -0.5)
    s5_d = nrm(ks[11], (DEPTH, S5_WIDTH), 1.0)
    s5_glu_w = nrm(ks[12], (DEPTH, S5_WIDTH, S5_WIDTH), S5_WIDTH ** -0.5)
    s5_glu_b = nrm(ks[13], (DEPTH, S5_WIDTH), 0.01)
    s5_proj = nrm(ks[14], (DEPTH, S5_WIDTH, D_MODEL), S5_WIDTH ** -0.5)
    lru_conv_w = nrm(ks[15], (DEPTH, CONV_WIDTH, LRU_WIDTH), CONV_WIDTH ** -0.5)
    lru_conv_b = nrm(ks[16], (DEPTH, LRU_WIDTH), 0.01)
    a_target = jax.random.uniform(ks[17], (DEPTH, N_DIR, LRU_WIDTH), f32, 0.9, 0.999)
    s = a_target ** (1.0 / RG_C)
    lru_lambda = jnp.log(s) - jnp.log1p(-s)
    lru_wa = nrm(ks[18], (DEPTH, N_DIR, LRU_HEADS, LRU_BLOCK, LRU_BLOCK), LRU_BLOCK ** -0.5)
    lru_ba = nrm(ks[19], (DEPTH, N_DIR, LRU_WIDTH), 0.01)
    lru_wx = nrm(ks[20], (DEPTH, N_DIR, LRU_HEADS, LRU_BLOCK, LRU_BLOCK), LRU_BLOCK ** -0.5)
    lru_bx = nrm(ks[21], (DEPTH, N_DIR, LRU_WIDTH), 0.01)
    lru_proj = nrm(ks[22], (DEPTH, LRU_WIDTH, D_MODEL), LRU_WIDTH ** -0.5)
    w_out = nrm(ks[23], (DEPTH, D_MODEL, D_MODEL), D_MODEL ** -0.5)
    norm_f_g = 1.0 + nrm(ks[24], (D_MODEL,), 0.02)
    return {"x_prompt": x_prompt, "x_sample": x_sample, "norm_g": norm_g, "w_in": w_in,
            "s5_a_re": s5_a_re, "s5_a_im": s5_a_im, "s5_log_dt": s5_log_dt,
            "s5_b_re": s5_b_re, "s5_b_im": s5_b_im, "s5_c_re": s5_c_re, "s5_c_im": s5_c_im,
            "s5_d": s5_d, "s5_glu_w": s5_glu_w, "s5_glu_b": s5_glu_b, "s5_proj": s5_proj,
            "lru_conv_w": lru_conv_w, "lru_conv_b": lru_conv_b, "lru_lambda": lru_lambda,
            "lru_wa": lru_wa, "lru_ba": lru_ba, "lru_wx": lru_wx, "lru_bx": lru_bx,
            "lru_proj": lru_proj, "w_out": w_out, "norm_f_g": norm_f_g}


def reference(x_prompt, x_sample, norm_g, w_in, s5_a_re, s5_a_im, s5_log_dt, s5_b_re, s5_b_im,
              s5_c_re, s5_c_im, s5_d, s5_glu_w, s5_glu_b, s5_proj, lru_conv_w, lru_conv_b,
              lru_lambda, lru_wa, lru_ba, lru_wx, lru_bx, lru_proj, w_out, norm_f_g):
    y_prompt = _trunk(x_prompt, norm_g, w_in, s5_a_re, s5_a_im, s5_log_dt, s5_b_re, s5_b_im,
                      s5_c_re, s5_c_im, s5_d, s5_glu_w, s5_glu_b, s5_proj, lru_conv_w, lru_conv_b,
                      lru_lambda, lru_wa, lru_ba, lru_wx, lru_bx, lru_proj, w_out, norm_f_g)
    y_sample = _trunk(x_sample, norm_g, w_in, s5_a_re, s5_a_im, s5_log_dt, s5_b_re, s5_b_im,
                      s5_c_re, s5_c_im, s5_d, s5_glu_w, s5_glu_b, s5_proj, lru_conv_w, lru_conv_b,
                      lru_lambda, lru_wa, lru_ba, lru_wx, lru_bx, lru_proj, w_out, norm_f_g)
    return (y_prompt, y_sample)
```

```python
import functools
import math

import jax
import jax.numpy as jnp
from jax import lax
from jax.experimental import pallas as pl
from jax.experimental.pallas import tpu as pltpu

F32 = jnp.float32
BF16 = jnp.bfloat16

D_MODEL = 1024
SUB = 16
S5_GROUPS = 64
S5_GROUP_SIZE = 16
S5_STATE = 64
GROUPS_PER_STEP = 8
LRU_HEADS = 8
LRU_BLOCK = 128
RG_C = 8.0
EPS = 1e-6
LANES = 128
SUBLANES = 8
Q_S5 = 128
R_BLK = 4
Q_IN = 16
Q_LRU_B = 64
Q_OUT = 32
VMEM_LIMIT = 56 * 1024 * 1024

T_STEP = 0
T_BLK = 24
T_FIX = 88
T_A16 = 96
T_ROWS = 104


def _rms(x, g):
    return x * lax.rsqrt(jnp.mean(x * x, axis=-1, keepdims=True) + EPS) * g


def _cmul_cat(c, x):
    cre, cim = c[:, :LANES], c[:, LANES:]
    xre, xim = x[:, :LANES], x[:, LANES:]
    return jnp.concatenate([cre * xre - cim * xim, cre * xim + cim * xre], axis=1)


def _in_t_kernel(x_ref, g_ref, w_ref, o_ref):
    hs = []
    for rr in range(R_BLK):
        xr = x_ref[0, :, rr * D_MODEL:(rr + 1) * D_MODEL]
        hs.append(_rms(xr, g_ref[...]).astype(BF16))
    h = jnp.concatenate(hs, axis=0)
    res = lax.dot_general(w_ref[...], h, (((1,), (1,)), ((), ())),
                          preferred_element_type=F32)
    for rr in range(R_BLK):
        o_ref[0, rr] = res[:, rr * Q_S5:(rr + 1) * Q_S5]


def _in_n_kernel(x_ref, g_ref, w_ref, o0, o1, o2, o3):
    hs = []
    for r in range(SUB):
        xr = x_ref[0, :, r * D_MODEL:(r + 1) * D_MODEL]
        hs.append(_rms(xr, g_ref[...]).astype(BF16))
    h = jnp.concatenate(hs, axis=0)
    for k, o in enumerate((o0, o1, o2, o3)):
        res = jnp.dot(h, w_ref[:, k * D_MODEL:(k + 1) * D_MODEL], preferred_element_type=F32)
        for r in range(SUB):
            o[0, :, r * D_MODEL:(r + 1) * D_MODEL] = res[r * Q_IN:(r + 1) * Q_IN, :]


def _s5_kernel(u_ref, d_ref, mt_ref, wbt_ref, wct_ref, tab_ref, o_ref,
               s_ref, cf_ref, cb_ref, hb_ref, *, nt):
    ph = pl.program_id(2)
    i = pl.program_id(3)
    tile = jnp.where(ph == 0, nt - 1 - i, i)
    q = Q_S5
    nblk = q // SUBLANES

    @pl.when(i == 0)
    def _():
        cf_ref[...] = jnp.zeros_like(cf_ref)
        cb_ref[...] = jnp.zeros_like(cb_ref)

    def lane_fwd(rows):
        lane = lax.broadcasted_iota(jnp.int32, (rows, 2 * LANES), 1)
        return (lane % LANES) < S5_STATE

    fwd8 = lane_fwd(SUBLANES)
    fwd16 = lane_fwd(nblk)
    row8 = lax.broadcasted_iota(jnp.int32, (SUBLANES, 2 * LANES), 0)
    row16 = lax.broadcasted_iota(jnp.int32, (nblk, 2 * LANES), 0)

    def group(gl, carry):
        c0 = pl.multiple_of(gl * S5_GROUP_SIZE, S5_GROUP_SIZE)
        xt = u_ref[0, :, pl.ds(c0, S5_GROUP_SIZE), :]
        xb = xt.reshape(SUB * S5_GROUP_SIZE, q).astype(BF16)
        st = jnp.dot(wbt_ref[gl], xb, preferred_element_type=F32)
        s_ref[...] = st.T

        @pl.when(ph == 0)
        def _():
            hb_ref[tile, gl] = cb_ref[gl]

        cf_in = cf_ref[gl]
        cb_in = hb_ref[tile, gl]
        a16 = tab_ref[gl, T_A16:T_A16 + SUBLANES, :]
        tf = _cmul_cat(a16, cf_in)
        tb = _cmul_cat(a16, cb_in)
        s_ref[0:SUBLANES, :] += jnp.where(fwd8 & (row8 == 0), tf, 0.0)
        s_ref[q - SUBLANES:q, :] += jnp.where((~fwd8) & (row8 == SUBLANES - 1), tb, 0.0)

        steps = [tab_ref[gl, T_STEP + 8 * si:T_STEP + 8 * si + 8, :] for si in range(3)]
        ends = []
        for k in range(nblk):
            b = s_ref[k * SUBLANES:(k + 1) * SUBLANES, :]
            for si, s in enumerate((1, 2, 4)):
                sh = jnp.where(fwd8, pltpu.roll(b, s, 0), pltpu.roll(b, SUBLANES - s, 0))
                b = b + _cmul_cat(steps[si], sh)
            s_ref[k * SUBLANES:(k + 1) * SUBLANES, :] = b
            ends.append(jnp.where(fwd8[0:1], b[SUBLANES - 1:SUBLANES, :], b[0:1, :]))

        e = jnp.concatenate(ends, axis=0)
        for si, s in enumerate((1, 2, 4, 8)):
            sh = jnp.where(fwd16, pltpu.roll(e, s, 0), pltpu.roll(e, nblk - s, 0))
            e = e + _cmul_cat(tab_ref[gl, T_BLK + 16 * si:T_BLK + 16 * si + 16, :], sh)
        eprev = jnp.where(fwd16,
                          jnp.where(row16 >= 1, pltpu.roll(e, 1, 0), 0.0),
                          jnp.where(row16 <= nblk - 2, pltpu.roll(e, nblk - 1, 0), 0.0))

        fix = tab_ref[gl, T_FIX:T_FIX + SUBLANES, :]
        for k in range(nblk):
            ev = jnp.broadcast_to(eprev[k:k + 1, :], (SUBLANES, 2 * LANES))
            s_ref[k * SUBLANES:(k + 1) * SUBLANES, :] += _cmul_cat(fix, ev)

        new_cf = jnp.broadcast_to(e[nblk - 1:nblk, :], (SUBLANES, 2 * LANES))
        new_cb = jnp.broadcast_to(e[0:1, :], (SUBLANES, 2 * LANES))

        @pl.when(ph == 0)
        def _():
            cb_ref[gl] = new_cb

        @pl.when(ph == 1)
        def _():
            cf_ref[gl] = new_cf
            h = s_ref[...]
            rowq = lax.broadcasted_iota(jnp.int32, (q, 2 * LANES), 0)
            dn = jnp.where(rowq == 0, jnp.broadcast_to(cf_in[0:1, :], (q, 2 * LANES)),
                           pltpu.roll(h, 1, 0))
            up = jnp.where(rowq == q - 1, jnp.broadcast_to(cb_in[0:1, :], (q, 2 * LANES)),
                           pltpu.roll(h, q - 1, 0))
            hprev = jnp.where(lane_fwd(q), dn, up).astype(BF16)
            y = jnp.dot(mt_ref[gl], xb, preferred_element_type=F32)
            y = y + lax.dot_general(wct_ref[gl], hprev, (((1,), (1,)), ((), ())),
                                    preferred_element_type=F32)
            dv = d_ref[pl.ds(c0, S5_GROUP_SIZE), :]
            o_ref[0, :, pl.ds(c0, S5_GROUP_SIZE), :] = (
                y.reshape(SUB, S5_GROUP_SIZE, q) + xt * dv[None])
        return carry

    lax.fori_loop(0, GROUPS_PER_STEP, group, 0)


def _s5_post_kernel(y_ref, z_ref, g_ref, wg_ref, bg_ref, wp_ref, o_ref):
    ys = [jax.nn.gelu(y_ref[0, rr]) for rr in range(R_BLK)]
    y = jnp.concatenate(ys, axis=1)
    glu = jnp.dot(wg_ref[...], y.astype(BF16), preferred_element_type=F32)
    bias = jnp.concatenate([bg_ref[...]] * R_BLK, axis=1)
    y = y * jax.nn.sigmoid(glu + bias)
    z = jnp.concatenate([z_ref[0, rr] for rr in range(R_BLK)], axis=1)
    y = (y * jax.nn.silu(z)).astype(BF16)
    ys5 = lax.dot_general(y, wp_ref[...], (((0,), (0,)), ((), ())),
                          preferred_element_type=F32)
    for rr in range(R_BLK):
        g = g_ref[0, :, rr * D_MODEL:(rr + 1) * D_MODEL]
        o_ref[0, :, rr * D_MODEL:(rr + 1) * D_MODEL] = (
            jax.nn.sigmoid(g) * ys5[rr * Q_S5:(rr + 1) * Q_S5, :])


def _lru_gates(xc_ref, hd, wa_ref, ba_ref, wx_ref, bx_ref, cv_ref, nq):
    c0 = pl.multiple_of(hd * LRU_BLOCK, LRU_BLOCK)
    xs = [xc_ref[0, :, pl.ds(pl.multiple_of(r * D_MODEL + c0, LRU_BLOCK), LRU_BLOCK)]
          for r in range(SUB)]
    xc = jnp.concatenate(xs, axis=0)
    xb = xc.astype(BF16)
    ra = jnp.dot(xb, wa_ref[hd], preferred_element_type=F32) + ba_ref[hd]
    rx = jnp.dot(xb, wx_ref[hd], preferred_element_type=F32) + bx_ref[hd]
    log_a = cv_ref[hd] * jax.nn.sigmoid(ra)
    a = jnp.exp(log_a)
    b = jnp.sqrt(1.0 - a * a) * (jax.nn.sigmoid(rx) * xc)
    return a, b


def _lru_scan(a, b, h_ref, col_of, p_ref, carry_row, nq, reverse):
    order = list(range(SUB - 1, -1, -1)) if reverse else list(range(SUB))
    h = None
    p = None
    for r in order:
        ar = a[r * nq:(r + 1) * nq, :]
        br = b[r * nq:(r + 1) * nq, :]
        h = br if h is None else ar * h + br
        p = ar if p is None else ar * p
        h_ref[0, :, col_of(r)] = h
        p_ref[r] = p

    row = lax.broadcasted_iota(jnp.int32, (nq, LRU_BLOCK), 0)
    s = 1
    while s < nq:
        keep = (row < nq - s) if reverse else (row >= s)
        shift = (nq - s) if reverse else s
        hs = jnp.where(keep, pltpu.roll(h, shift, 0), 0.0)
        ps = jnp.where(keep, pltpu.roll(p, shift, 0), 1.0)
        h = h + p * hs
        p = p * ps
        s *= 2
    e = h + p * carry_row
    if reverse:
        cin = jnp.where(row < nq - 1, pltpu.roll(e, nq - 1, 0), carry_row)
        carry_out = e[0:1, :]
    else:
        cin = jnp.where(row >= 1, pltpu.roll(e, 1, 0), carry_row)
        carry_out = e[nq - 1:nq, :]
    for r in order:
        h_ref[0, :, col_of(r)] = h_ref[0, :, col_of(r)] + p_ref[r] * cin
    return carry_out


def _lru_bwd_kernel(u_ref, halo_ref, cw_ref, cbias_ref, wa_ref, ba_ref, wx_ref, bx_ref, cv_ref,
                    xc_ref, h_ref, p_ref, nx_ref, carry_ref, *, nt):
    i = pl.program_id(1)
    tile = nt - 1 - i
    nq = Q_LRU_B

    @pl.when(i == 0)
    def _():
        nx_ref[...] = jnp.zeros_like(nx_ref)
        carry_ref[...] = jnp.zeros_like(carry_ref)

    def cls(r):
        return u_ref[0, :, r * D_MODEL:(r + 1) * D_MODEL]

    row = lax.broadcasted_iota(jnp.int32, (nq, D_MODEL), 0)
    live = (tile > 0).astype(F32)
    hp = halo_ref[0, SUBLANES - 1:SUBLANES, :] * live
    d14 = jnp.where(row == 0, hp[:, 14 * D_MODEL:15 * D_MODEL], pltpu.roll(cls(14), 1, 0))
    d15 = jnp.where(row == 0, hp[:, 15 * D_MODEL:16 * D_MODEL], pltpu.roll(cls(15), 1, 0))
    u0 = jnp.where(row == nq - 1, nx_ref[0:1, :], pltpu.roll(cls(0), nq - 1, 0))
    nx_new = u_ref[0, 0:1, 0:D_MODEL]

    def tap(r):
        if r == -2:
            return d14
        if r == -1:
            return d15
        if r == SUB:
            return u0
        return cls(r)

    w = cw_ref[...]
    for r in range(SUB):
        xc = (cbias_ref[...] + tap(r - 2) * w[0:1] + tap(r - 1) * w[1:2]
              + tap(r) * w[2:3] + tap(r + 1) * w[3:4])
        xc_ref[0, :, r * D_MODEL:(r + 1) * D_MODEL] = xc
    nx_ref[0:1, :] = nx_new

    def head(hd, c):
        c0 = pl.multiple_of(hd * LRU_BLOCK, LRU_BLOCK)
        a, b = _lru_gates(xc_ref, hd, wa_ref, ba_ref, wx_ref, bx_ref, cv_ref, nq)

        def col_of(r):
            return pl.ds(pl.multiple_of(r * D_MODEL + c0, LRU_BLOCK), LRU_BLOCK)
        cout = _lru_scan(a, b, h_ref, col_of, p_ref,
                         carry_ref[0:1, pl.ds(c0, LRU_BLOCK)], nq, True)
        carry_ref[0:1, pl.ds(c0, LRU_BLOCK)] = cout
        return c
    lax.fori_loop(0, LRU_HEADS, head, 0)


def _out_kernel(x_ref, xc_ref, hb_ref, z_ref, g_ref, m_ref, wa_ref, ba_ref, wx_ref, bx_ref, cv_ref,
                wl_ref, wo_ref, gf_ref, o_ref, hf_ref, p_ref, carry_ref, v_ref):
    i = pl.program_id(1)
    nq = Q_OUT

    @pl.when(i == 0)
    def _():
        carry_ref[...] = jnp.zeros_like(carry_ref)

    def head(hd, c):
        c0 = pl.multiple_of(hd * LRU_BLOCK, LRU_BLOCK)
        a, b = _lru_gates(xc_ref, hd, wa_ref, ba_ref, wx_ref, bx_ref, cv_ref, nq)

        def col_of(r):
            return pl.ds(pl.multiple_of(r * D_MODEL + c0, LRU_BLOCK), LRU_BLOCK)
        cout = _lru_scan(a, b, hf_ref, col_of, p_ref,
                         carry_ref[0:1, pl.ds(c0, LRU_BLOCK)], nq, False)
        carry_ref[0:1, pl.ds(c0, LRU_BLOCK)] = cout
        for r in range(SUB):
            hl = hf_ref[0, :, col_of(r)] + hb_ref[0, :, col_of(r)]
            v = hl * jax.nn.silu(z_ref[0, :, col_of(r)])
            v_ref[r * nq:(r + 1) * nq, pl.ds(c0, LRU_BLOCK)] = v.astype(BF16)
        return c
    lax.fori_loop(0, LRU_HEADS, head, 0)

    ylru = jnp.dot(v_ref[...], wl_ref[...], preferred_element_type=F32)
    ms = []
    for r in range(SUB):
        sl = slice(r * D_MODEL, (r + 1) * D_MODEL)
        ms.append((m_ref[0, :, sl] + jax.nn.sigmoid(g_ref[0, :, sl]) * ylru[r * nq:(r + 1) * nq, :])
                  .astype(BF16))
    dm = jnp.dot(jnp.concatenate(ms, axis=0), wo_ref[...], preferred_element_type=F32)
    for r in range(SUB):
        sl = slice(r * D_MODEL, (r + 1) * D_MODEL)
        xo = x_ref[0, :, sl] + dm[r * nq:(r + 1) * nq, :]
        o_ref[0, :, sl] = _rms(xo, gf_ref[...])


def _prep_s5(a_re, a_im, log_dt, b_re, b_im, c_re, c_im):
    hi = lax.Precision.HIGHEST
    g = S5_GROUPS
    lam = lax.complex(a_re.astype(F32), a_im.astype(F32))
    ldt = lam * jnp.exp(log_dt.astype(F32))[..., None]
    abar = jnp.exp(ldt)
    bbar = ((abar - 1.0) / lam)[..., None] * lax.complex(b_re.astype(F32), b_im.astype(F32))
    cmat = lax.complex(c_re.astype(F32), c_im.astype(F32))

    def apow(n):
        return jnp.exp(ldt[:, :, None, :] * n.astype(F32)[None, None, :, None])

    k16 = jnp.arange(SUB)
    pw = apow(jnp.arange(SUB + 1))
    kern = jnp.real(jnp.einsum('dgip,dgkp,dgpj->dgkij', cmat, pw[:, :, :SUB], bbar, precision=hi))
    kf, kb = kern[0], kern[1]
    rp, rr = k16[:, None], k16[None, :]
    idx = jnp.abs(rp - rr)
    kfe, kbe = kf[:, idx], kb[:, idx]
    up = (rp > rr)[None, :, :, None, None]
    lo = (rp < rr)[None, :, :, None, None]
    m = jnp.where(up, kfe, jnp.where(lo, kbe, kfe + kbe))
    mt = m.transpose(0, 1, 3, 2, 4).reshape(g, 256, 256)

    def cat(zf, zb, sign=1.0):
        return jnp.concatenate([jnp.real(zf), jnp.real(zb), sign * jnp.imag(zf), sign * jnp.imag(zb)],
                               axis=-1)

    wf = pw[0][:, ::-1][:, 1:, None, :] * bbar[0].transpose(0, 2, 1)[:, None]
    wb = pw[1][:, :SUB, None, :] * bbar[1].transpose(0, 2, 1)[:, None]
    wbt = cat(wf, wb).reshape(g, 256, 256).transpose(0, 2, 1)
    cf = cmat[0][:, None] * pw[0][:, 1:, None, :]
    cb = cmat[1][:, None] * pw[1][:, ::-1][:, :SUB, None, :]
    wct = cat(cf, cb, -1.0).reshape(g, 256, 256)

    def tab(nf, nb, mf, mb):
        zf = apow(nf)[0] * mf[None, :, None]
        zb = apow(nb)[1] * mb[None, :, None]
        return cat(zf, zb)

    m8 = jnp.arange(SUBLANES)
    nblk = Q_S5 // SUBLANES
    m16 = jnp.arange(nblk)
    parts = []
    for s in (1, 2, 4):
        n = jnp.full((SUBLANES,), SUB * s)
        parts.append(tab(n, n, (m8 >= s).astype(F32), (m8 < SUBLANES - s).astype(F32)))
    for s in (1, 2, 4, 8):
        n = jnp.full((nblk,), SUB * SUBLANES * s)
        parts.append(tab(n, n, (m16 >= s).astype(F32), (m16 < nblk - s).astype(F32)))
    one8 = jnp.ones((SUBLANES,), F32)
    parts.append(tab(SUB * (m8 + 1), SUB * (SUBLANES - m8), one8, one8))
    n = jnp.full((SUBLANES,), SUB)
    parts.append(tab(n, n, one8, one8))
    table = jnp.concatenate(parts, axis=1)
    return mt.astype(BF16), wbt.astype(BF16), wct.astype(BF16), table


def _full(shape):
    return pl.BlockSpec(shape, lambda *_: (0,) * len(shape))


def _params(sem, **kw):
    return pltpu.CompilerParams(dimension_semantics=sem, vmem_limit_bytes=VMEM_LIMIT, **kw)


def _trunk(x, w):
    bsz, seq, _ = x.shape
    lq = seq // SUB
    assert seq % (SUB * Q_S5) == 0
    x2 = x.reshape(bsz, lq, SUB * D_MODEL)
    row_shape = jax.ShapeDtypeStruct((bsz, lq, SUB * D_MODEL), F32)

    nt = lq // Q_S5
    uz = pl.pallas_call(
        _in_t_kernel,
        grid=(bsz, nt, SUB // R_BLK),
        in_specs=[pl.BlockSpec((1, Q_S5, R_BLK * D_MODEL), lambda b, t, r: (b, t, r)),
                  _full((1, D_MODEL)), _full((2 * D_MODEL, D_MODEL))],
        out_specs=pl.BlockSpec((1, R_BLK, 2 * D_MODEL, Q_S5), lambda b, t, r: (b, r, 0, t)),
        out_shape=jax.ShapeDtypeStruct((bsz, SUB, 2 * D_MODEL, lq), F32),
        compiler_params=_params(("parallel", "parallel", "parallel")),
        name="in_proj_t",
    )(x2, w["norm_g"], w["w_s5_t"])

    rows = pl.BlockSpec((1, Q_IN, SUB * D_MODEL), lambda b, t: (b, t, 0))
    u_lru, z_lru, g_s5, g_lru = pl.pallas_call(
        _in_n_kernel,
        grid=(bsz, lq // Q_IN),
        in_specs=[rows, _full((1, D_MODEL)), _full((D_MODEL, 4 * D_MODEL))],
        out_specs=[rows] * 4,
        out_shape=[row_shape] * 4,
        compiler_params=_params(("parallel", "parallel")),
        name="in_proj_n",
    )(x2, w["norm_g"], w["w_rest"])

    def tile_of(ph, i):
        return jnp.where(ph == 0, nt - 1 - i, i)
    gw = pl.BlockSpec((GROUPS_PER_STEP, 256, 256), lambda g, b, ph, i: (g, 0, 0))
    y_t = pl.pallas_call(
        functools.partial(_s5_kernel, nt=nt),
        grid=(S5_GROUPS // GROUPS_PER_STEP, bsz, 2, nt),
        in_specs=[pl.BlockSpec((1, SUB, LANES, Q_S5), lambda g, b, ph, i: (b, 0, g, tile_of(ph, i))),
                  pl.BlockSpec((LANES, LANES), lambda g, b, ph, i: (g, 0)),
                  gw, gw, gw,
                  pl.BlockSpec((GROUPS_PER_STEP, T_ROWS, 2 * LANES), lambda g, b, ph, i: (g, 0, 0))],
        out_specs=pl.BlockSpec((1, SUB, LANES, Q_S5), lambda g, b, ph, i: (b, 0, g, ph * i)),
        out_shape=jax.ShapeDtypeStruct((bsz, SUB, D_MODEL, lq), F32),
        scratch_shapes=[pltpu.VMEM((Q_S5, 2 * LANES), F32),
                        pltpu.VMEM((GROUPS_PER_STEP, SUBLANES, 2 * LANES), F32),
                        pltpu.VMEM((GROUPS_PER_STEP, SUBLANES, 2 * LANES), F32),
                        pltpu.VMEM((nt, GROUPS_PER_STEP, SUBLANES, 2 * LANES), F32)],
        compiler_params=_params(("arbitrary",) * 4),
        name="s5_ssm",
    )(uz, w["s5_d_b"], w["s5_mt"], w["s5_wbt"], w["s5_wct"], w["s5_tab"])

    m_s5 = pl.pallas_call(
        _s5_post_kernel,
        grid=(bsz, nt, SUB // R_BLK),
        in_specs=[pl.BlockSpec((1, R_BLK, D_MODEL, Q_S5), lambda b, t, r: (b, r, 0, t)),
                  pl.BlockSpec((1, R_BLK, D_MODEL, Q_S5), lambda b, t, r: (b, r, 1, t)),
                  pl.BlockSpec((1, Q_S5, R_BLK * D_MODEL), lambda b, t, r: (b, t, r)),
                  _full((D_MODEL, D_MODEL)), _full((D_MODEL, LANES)), _full((D_MODEL, D_MODEL))],
        out_specs=pl.BlockSpec((1, Q_S5, R_BLK * D_MODEL), lambda b, t, r: (b, t, r)),
        out_shape=row_shape,
        compiler_params=_params(("parallel", "parallel", "parallel")),
        name="s5_post",
    )(y_t, uz, g_s5, w["glu_wt"], w["glu_b_b"], w["s5_proj"])

    ntb = lq // Q_LRU_B
    hpb = Q_LRU_B // SUBLANES
    rows_b = pl.BlockSpec((1, Q_LRU_B, SUB * D_MODEL), lambda b, i: (b, ntb - 1 - i, 0))
    gate_w = _full((LRU_HEADS, LRU_BLOCK, LRU_BLOCK))
    gate_b = _full((LRU_HEADS, 1, LRU_BLOCK))
    xc, h_bwd = pl.pallas_call(
        functools.partial(_lru_bwd_kernel, nt=ntb),
        grid=(bsz, ntb),
        in_specs=[rows_b,
                  pl.BlockSpec((1, SUBLANES, SUB * D_MODEL),
                               lambda b, i: (b, jnp.maximum((ntb - 1 - i) * hpb - 1, 0), 0)),
                  _full((4, D_MODEL)), _full((1, D_MODEL)),
                  gate_w, gate_b, gate_w, gate_b, gate_b],
        out_specs=[rows_b, rows_b],
        out_shape=[row_shape, row_shape],
        scratch_shapes=[pltpu.VMEM((SUB, Q_LRU_B, LRU_BLOCK), F32),
                        pltpu.VMEM((SUBLANES, D_MODEL), F32),
                        pltpu.VMEM((SUBLANES, D_MODEL), F32)],
        compiler_params=_params(("arbitrary", "arbitrary")),
        name="lru_bwd",
    )(u_lru, u_lru, w["conv_w"], w["conv_b"], w["wa"][1], w["ba"][1], w["wx"][1], w["bx"][1],
      w["cvec"][1])

    rows_o = pl.BlockSpec((1, Q_OUT, SUB * D_MODEL), lambda b, i: (b, i, 0))
    out = pl.pallas_call(
        _out_kernel,
        grid=(bsz, lq // Q_OUT),
        in_specs=[rows_o] * 6 + [gate_w, gate_b, gate_w, gate_b, gate_b,
                                 _full((D_MODEL, D_MODEL)), _full((D_MODEL, D_MODEL)),
                                 _full((1, D_MODEL))],
        out_specs=rows_o,
        out_shape=row_shape,
        scratch_shapes=[pltpu.VMEM((1, Q_OUT, SUB * D_MODEL), F32),
                        pltpu.VMEM((SUB, Q_OUT, LRU_BLOCK), F32),
                        pltpu.VMEM((SUBLANES, D_MODEL), F32),
                        pltpu.VMEM((SUB * Q_OUT, D_MODEL), BF16)],
        compiler_params=_params(("arbitrary", "arbitrary")),
        name="lru_fwd_out",
    )(x2, xc, h_bwd, z_lru, g_lru, m_s5, w["wa"][0], w["ba"][0], w["wx"][0], w["bx"][0],
      w["cvec"][0], w["lru_proj"], w["w_out"], w["norm_f_g"])
    return out.reshape(bsz, seq, D_MODEL)


def kernel(x_prompt, x_sample, norm_g, w_in, s5_a_re, s5_a_im, s5_log_dt, s5_b_re, s5_b_im, s5_c_re, s5_c_im, s5_d, s5_glu_w, s5_glu_b, s5_proj, lru_conv_w, lru_conv_b, lru_lambda, lru_wa, lru_ba, lru_wx, lru_bx, lru_proj, w_out, norm_f_g):
    assert norm_g.shape[0] == 1, "single-layer trunk"
    wi = w_in[0]
    mt, wbt, wct, tab = _prep_s5(s5_a_re[0], s5_a_im[0], s5_log_dt[0], s5_b_re[0], s5_b_im[0],
                                 s5_c_re[0], s5_c_im[0])
    w = {
        "norm_g": norm_g[0].reshape(1, D_MODEL).astype(F32),
        "w_s5_t": wi[:, :2 * D_MODEL].T.astype(BF16),
        "w_rest": wi[:, 2 * D_MODEL:].astype(BF16),
        "s5_mt": mt, "s5_wbt": wbt, "s5_wct": wct, "s5_tab": tab,
        "s5_d_b": jnp.broadcast_to(s5_d[0].astype(F32)[:, None], (D_MODEL, LANES)),
        "glu_wt": s5_glu_w[0].T.astype(BF16),
        "glu_b_b": jnp.broadcast_to(s5_glu_b[0].astype(F32)[:, None], (D_MODEL, LANES)),
        "s5_proj": s5_proj[0].astype(BF16),
        "conv_w": lru_conv_w[0].astype(F32),
        "conv_b": lru_conv_b[0].reshape(1, D_MODEL).astype(F32),
        "wa": lru_wa[0].astype(BF16),
        "wx": lru_wx[0].astype(BF16),
        "ba": lru_ba[0].reshape(2, LRU_HEADS, 1, LRU_BLOCK).astype(F32),
        "bx": lru_bx[0].reshape(2, LRU_HEADS, 1, LRU_BLOCK).astype(F32),
        "cvec": (-RG_C * jax.nn.softplus(-lru_lambda[0].astype(F32))).reshape(2, LRU_HEADS, 1, LRU_BLOCK),
        "lru_proj": lru_proj[0].astype(BF16),
        "w_out": w_out[0].astype(BF16),
        "norm_f_g": norm_f_g.reshape(1, D_MODEL).astype(F32),
    }
    return (_trunk(x_prompt, w), _trunk(x_sample, w))
```

```python
import functools

import jax
import jax.numpy as jnp
from jax import lax
from jax.experimental import pallas as pl
from jax.experimental.pallas import tpu as pltpu

F32 = jnp.float32
BF16 = jnp.bfloat16

D_MODEL = 1024
SUB = 16
S5_GROUPS = 64
S5_GROUP_SIZE = 16
S5_STATE = 64
GROUPS_PER_STEP = 8
LRU_HEADS = 8
LRU_BLOCK = 128
RG_C = 8.0
EPS = 1e-6
LANES = 128
SUBLANES = 8
Q_S5 = 128
R_BLK = 4
Q_IN = 16
Q_LRU_B = 64
Q_OUT = 32
VMEM_LIMIT = 56 * 1024 * 1024

T_STEP = 0
T_BLK = 24
T_FIX = 88
T_A16 = 96
T_AQ = 104
T_ROWS = 112


def _rms(x, g):
    return x * lax.rsqrt(jnp.mean(x * x, axis=-1, keepdims=True) + EPS) * g


def _cmul_cat(c, x):
    cre, cim = c[:, :LANES], c[:, LANES:]
    xre, xim = x[:, :LANES], x[:, LANES:]
    return jnp.concatenate([cre * xre - cim * xim, cre * xim + cim * xre], axis=1)


def _in_t_kernel(x_ref, g_ref, w_ref, o_ref, h_ref):
    @pl.when(pl.program_id(2) == 0)
    def _():
        for r in range(SUB):
            h_ref[r * Q_S5:(r + 1) * Q_S5, :] = _rms(x_ref[0, :, r, :], g_ref[...]).astype(BF16)

    rows = R_BLK * Q_S5
    for rb in range(SUB // R_BLK):
        res = lax.dot_general(w_ref[...], h_ref[rb * rows:(rb + 1) * rows, :],
                              (((1,), (1,)), ((), ())), preferred_element_type=F32)
        for rr in range(R_BLK):
            o_ref[0, rb * R_BLK + rr] = res[:, rr * Q_S5:(rr + 1) * Q_S5]


def _in_n_kernel(x_ref, g_ref, w_ref, o0, o1, o2, o3):
    hs = [_rms(x_ref[0, :, r, :], g_ref[...]).astype(BF16) for r in range(SUB)]
    h = jnp.concatenate(hs, axis=0)
    for k, o in enumerate((o0, o1, o2, o3)):
        res = jnp.dot(h, w_ref[:, k * D_MODEL:(k + 1) * D_MODEL], preferred_element_type=F32)
        for r in range(SUB):
            o[0, :, r * D_MODEL:(r + 1) * D_MODEL] = res[r * Q_IN:(r + 1) * Q_IN, :]


def _s5_kernel(u_ref, d_ref, mt_ref, wbt_ref, wct_ref, tab_ref, wq_ref, o_ref,
               s_ref, cf_ref, cb_ref, hb_ref, *, nt):
    ph = pl.program_id(2)
    i = pl.program_id(3)
    tile = jnp.where(ph == 0, nt - 1 - i, i)
    q = Q_S5
    nblk = q // SUBLANES

    @pl.when(i == 0)
    def _():
        cf_ref[...] = jnp.zeros_like(cf_ref)
        cb_ref[...] = jnp.zeros_like(cb_ref)

    def lane_fwd(rows):
        lane = lax.broadcasted_iota(jnp.int32, (rows, 2 * LANES), 1)
        return (lane % LANES) < S5_STATE

    def load_x(gl):
        c0 = pl.multiple_of(gl * S5_GROUP_SIZE, S5_GROUP_SIZE)
        xt = u_ref[0, :, pl.ds(c0, S5_GROUP_SIZE), :]
        return c0, xt, xt.reshape(SUB * S5_GROUP_SIZE, q).astype(BF16)

    def states(gl, xb):
        st = jnp.dot(wbt_ref[gl], xb, preferred_element_type=F32)
        return st.T

    @pl.when(ph == 0)
    def _():
        def group(gl, carry):
            _, _, xb = load_x(gl)
            s = states(gl, xb)
            cb = cb_ref[gl]
            hb_ref[tile, gl] = cb
            ws = _cmul_cat(wq_ref[gl], s)
            red = jnp.sum(ws, axis=0, keepdims=True)
            cb_ref[gl] = (jnp.broadcast_to(red, (SUBLANES, 2 * LANES))
                          + _cmul_cat(tab_ref[gl, T_AQ:T_AQ + SUBLANES, :], cb))
            return carry
        lax.fori_loop(0, GROUPS_PER_STEP, group, 0, unroll=2)

    @pl.when(ph == 1)
    def _():
        fwd8 = lane_fwd(SUBLANES)
        fwd16 = lane_fwd(nblk)
        row8 = lax.broadcasted_iota(jnp.int32, (SUBLANES, 2 * LANES), 0)
        row16 = lax.broadcasted_iota(jnp.int32, (nblk, 2 * LANES), 0)
        rowq = lax.broadcasted_iota(jnp.int32, (q, 2 * LANES), 0)
        fwdq = lane_fwd(q)

        def group(gl, sc):
            c0, xt, xb = load_x(gl)
            sc[...] = states(gl, xb)
            cf_in = cf_ref[gl]
            cb_in = hb_ref[tile, gl]
            a16 = tab_ref[gl, T_A16:T_A16 + SUBLANES, :]
            sc[0:SUBLANES, :] += jnp.where(fwd8 & (row8 == 0), _cmul_cat(a16, cf_in), 0.0)
            sc[q - SUBLANES:q, :] += jnp.where((~fwd8) & (row8 == SUBLANES - 1),
                                               _cmul_cat(a16, cb_in), 0.0)

            steps = [tab_ref[gl, T_STEP + 8 * si:T_STEP + 8 * si + 8, :] for si in range(3)]
            ends = []
            for k in range(nblk):
                b = sc[k * SUBLANES:(k + 1) * SUBLANES, :]
                for si, s in enumerate((1, 2, 4)):
                    sh = jnp.where(fwd8, pltpu.roll(b, s, 0), pltpu.roll(b, SUBLANES - s, 0))
                    b = b + _cmul_cat(steps[si], sh)
                sc[k * SUBLANES:(k + 1) * SUBLANES, :] = b
                ends.append(jnp.where(fwd8[0:1], b[SUBLANES - 1:SUBLANES, :], b[0:1, :]))

            e = jnp.concatenate(ends, axis=0)
            for si, s in enumerate((1, 2, 4, 8)):
                sh = jnp.where(fwd16, pltpu.roll(e, s, 0), pltpu.roll(e, nblk - s, 0))
                e = e + _cmul_cat(tab_ref[gl, T_BLK + 16 * si:T_BLK + 16 * si + 16, :], sh)
            eprev = jnp.where(fwd16,
                              jnp.where(row16 >= 1, pltpu.roll(e, 1, 0), 0.0),
                              jnp.where(row16 <= nblk - 2, pltpu.roll(e, nblk - 1, 0), 0.0))
            fix = tab_ref[gl, T_FIX:T_FIX + SUBLANES, :]
            for k in range(nblk):
                ev = jnp.broadcast_to(eprev[k:k + 1, :], (SUBLANES, 2 * LANES))
                sc[k * SUBLANES:(k + 1) * SUBLANES, :] += _cmul_cat(fix, ev)
            cf_ref[gl] = jnp.broadcast_to(e[nblk - 1:nblk, :], (SUBLANES, 2 * LANES))

            h = sc[...]
            dn = jnp.where(rowq == 0, jnp.broadcast_to(cf_in[0:1, :], (q, 2 * LANES)),
                           pltpu.roll(h, 1, 0))
            up = jnp.where(rowq == q - 1, jnp.broadcast_to(cb_in[0:1, :], (q, 2 * LANES)),
                           pltpu.roll(h, q - 1, 0))
            hprev = jnp.where(fwdq, dn, up).astype(BF16)
            y = jnp.dot(mt_ref[gl], xb, preferred_element_type=F32)
            y = y + lax.dot_general(wct_ref[gl], hprev, (((1,), (1,)), ((), ())),
                                    preferred_element_type=F32)
            dv = d_ref[pl.ds(c0, S5_GROUP_SIZE), :]
            o_ref[0, :, pl.ds(c0, S5_GROUP_SIZE), :] = (
                y.reshape(SUB, S5_GROUP_SIZE, q) + xt * dv[None])

        def pair(j, carry):
            group(2 * j, s_ref.at[0])
            group(2 * j + 1, s_ref.at[1])
            return carry
        lax.fori_loop(0, GROUPS_PER_STEP // 2, pair, 0)


def _s5_post_kernel(y_ref, z_ref, g_ref, wg_ref, bg_ref, wp_ref, o_ref):
    ys = [jax.nn.gelu(y_ref[0, rr]) for rr in range(R_BLK)]
    y = jnp.concatenate(ys, axis=1)
    glu = jnp.dot(wg_ref[...], y.astype(BF16), preferred_element_type=F32)
    bias = jnp.concatenate([bg_ref[...]] * R_BLK, axis=1)
    y = y * jax.nn.sigmoid(glu + bias)
    z = jnp.concatenate([z_ref[0, rr] for rr in range(R_BLK)], axis=1)
    y = (y * jax.nn.silu(z)).astype(BF16)
    ys5 = lax.dot_general(y, wp_ref[...], (((0,), (0,)), ((), ())),
                          preferred_element_type=F32)
    for rr in range(R_BLK):
        g = g_ref[0, :, rr * D_MODEL:(rr + 1) * D_MODEL]
        o_ref[0, :, rr * D_MODEL:(rr + 1) * D_MODEL] = (
            jax.nn.sigmoid(g) * ys5[rr * Q_S5:(rr + 1) * Q_S5, :])


def _lru_gates(xc_ref, hd, wa_ref, ba_ref, wx_ref, bx_ref, cv_ref, nq):
    c0 = pl.multiple_of(hd * LRU_BLOCK, LRU_BLOCK)
    xs = [xc_ref[0, :, pl.ds(pl.multiple_of(r * D_MODEL + c0, LRU_BLOCK), LRU_BLOCK)]
          for r in range(SUB)]
    xc = jnp.concatenate(xs, axis=0)
    xb = xc.astype(BF16)
    ra = jnp.dot(xb, wa_ref[hd], preferred_element_type=F32) + ba_ref[hd]
    rx = jnp.dot(xb, wx_ref[hd], preferred_element_type=F32) + bx_ref[hd]
    log_a = cv_ref[hd] * jax.nn.sigmoid(ra)
    a = jnp.exp(log_a)
    b = jnp.sqrt(1.0 - a * a) * (jax.nn.sigmoid(rx) * xc)
    return a, b


def _lru_scan(a, b, h_ref, col_of, p_ref, carry_row, nq, reverse):
    order = list(range(SUB - 1, -1, -1)) if reverse else list(range(SUB))
    h = None
    p = None
    for r in order:
        ar = a[r * nq:(r + 1) * nq, :]
        br = b[r * nq:(r + 1) * nq, :]
        h = br if h is None else ar * h + br
        p = ar if p is None else ar * p
        h_ref[0, :, col_of(r)] = h
        p_ref[r] = p

    row = lax.broadcasted_iota(jnp.int32, (nq, LRU_BLOCK), 0)
    s = 1
    while s < nq:
        keep = (row < nq - s) if reverse else (row >= s)
        shift = (nq - s) if reverse else s
        hs = jnp.where(keep, pltpu.roll(h, shift, 0), 0.0)
        ps = jnp.where(keep, pltpu.roll(p, shift, 0), 1.0)
        h = h + p * hs
        p = p * ps
        s *= 2
    e = h + p * carry_row
    if reverse:
        cin = jnp.where(row < nq - 1, pltpu.roll(e, nq - 1, 0), carry_row)
        carry_out = e[0:1, :]
    else:
        cin = jnp.where(row >= 1, pltpu.roll(e, 1, 0), carry_row)
        carry_out = e[nq - 1:nq, :]
    for r in order:
        h_ref[0, :, col_of(r)] = h_ref[0, :, col_of(r)] + p_ref[r] * cin
    return carry_out


def _lru_bwd_kernel(u_ref, halo_ref, cw_ref, cbias_ref, wa_ref, ba_ref, wx_ref, bx_ref, cv_ref,
                    xc_ref, h_ref, p_ref, nx_ref, carry_ref, *, nt):
    i = pl.program_id(1)
    tile = nt - 1 - i
    nq = Q_LRU_B

    @pl.when(i == 0)
    def _():
        nx_ref[...] = jnp.zeros_like(nx_ref)
        carry_ref[...] = jnp.zeros_like(carry_ref)

    def cls(r):
        return u_ref[0, :, r * D_MODEL:(r + 1) * D_MODEL]

    row = lax.broadcasted_iota(jnp.int32, (nq, D_MODEL), 0)
    live = (tile > 0).astype(F32)
    hp = halo_ref[0, SUBLANES - 1:SUBLANES, :] * live
    d14 = jnp.where(row == 0, hp[:, 14 * D_MODEL:15 * D_MODEL], pltpu.roll(cls(14), 1, 0))
    d15 = jnp.where(row == 0, hp[:, 15 * D_MODEL:16 * D_MODEL], pltpu.roll(cls(15), 1, 0))
    u0 = jnp.where(row == nq - 1, nx_ref[0:1, :], pltpu.roll(cls(0), nq - 1, 0))
    nx_new = u_ref[0, 0:1, 0:D_MODEL]

    def tap(r):
        if r == -2:
            return d14
        if r == -1:
            return d15
        if r == SUB:
            return u0
        return cls(r)

    w = cw_ref[...]
    for r in range(SUB):
        xc = (cbias_ref[...] + tap(r - 2) * w[0:1] + tap(r - 1) * w[1:2]
              + tap(r) * w[2:3] + tap(r + 1) * w[3:4])
        xc_ref[0, :, r * D_MODEL:(r + 1) * D_MODEL] = xc
    nx_ref[0:1, :] = nx_new

    def head(hd, c):
        c0 = pl.multiple_of(hd * LRU_BLOCK, LRU_BLOCK)
        a, b = _lru_gates(xc_ref, hd, wa_ref, ba_ref, wx_ref, bx_ref, cv_ref, nq)

        def col_of(r):
            return pl.ds(pl.multiple_of(r * D_MODEL + c0, LRU_BLOCK), LRU_BLOCK)
        cout = _lru_scan(a, b, h_ref, col_of, p_ref,
                         carry_ref[0:1, pl.ds(c0, LRU_BLOCK)], nq, True)
        carry_ref[0:1, pl.ds(c0, LRU_BLOCK)] = cout
        return c
    lax.fori_loop(0, LRU_HEADS, head, 0)


def _out_kernel(x_ref, xc_ref, hb_ref, z_ref, g_ref, m_ref, wa_ref, ba_ref, wx_ref, bx_ref, cv_ref,
                wl_ref, wo_ref, gf_ref, o_ref, hf_ref, p_ref, carry_ref, v_ref):
    i = pl.program_id(1)
    nq = Q_OUT

    @pl.when(i == 0)
    def _():
        carry_ref[...] = jnp.zeros_like(carry_ref)

    def head(hd, c):
        c0 = pl.multiple_of(hd * LRU_BLOCK, LRU_BLOCK)
        a, b = _lru_gates(xc_ref, hd, wa_ref, ba_ref, wx_ref, bx_ref, cv_ref, nq)

        def col_of(r):
            return pl.ds(pl.multiple_of(r * D_MODEL + c0, LRU_BLOCK), LRU_BLOCK)
        cout = _lru_scan(a, b, hf_ref, col_of, p_ref,
                         carry_ref[0:1, pl.ds(c0, LRU_BLOCK)], nq, False)
        carry_ref[0:1, pl.ds(c0, LRU_BLOCK)] = cout
        for r in range(SUB):
            hl = hf_ref[0, :, col_of(r)] + hb_ref[0, :, col_of(r)]
            v = hl * jax.nn.silu(z_ref[0, :, col_of(r)])
            v_ref[r * nq:(r + 1) * nq, pl.ds(c0, LRU_BLOCK)] = v.astype(BF16)
        return c
    lax.fori_loop(0, LRU_HEADS, head, 0)

    ylru = jnp.dot(v_ref[...], wl_ref[...], preferred_element_type=F32)
    ms = []
    for r in range(SUB):
        sl = slice(r * D_MODEL, (r + 1) * D_MODEL)
        ms.append((m_ref[0, :, sl] + jax.nn.sigmoid(g_ref[0, :, sl]) * ylru[r * nq:(r + 1) * nq, :])
                  .astype(BF16))
    dm = jnp.dot(jnp.concatenate(ms, axis=0), wo_ref[...], preferred_element_type=F32)
    for r in range(SUB):
        xo = x_ref[0, :, r, :] + dm[r * nq:(r + 1) * nq, :]
        o_ref[0, :, r, :] = _rms(xo, gf_ref[...])


def _cx_mul(a, b):
    return a[0] * b[0] - a[1] * b[1], a[0] * b[1] + a[1] * b[0]


def _prep_s5(a_re, a_im, log_dt, b_re, b_im, c_re, c_im):
    hi = lax.Precision.HIGHEST
    g = S5_GROUPS
    are, aim = a_re.astype(F32), a_im.astype(F32)
    dt = jnp.exp(log_dt.astype(F32))[..., None]
    lre, lim = are * dt, aim * dt

    def apow(n):
        nn = n.astype(F32)[None, None, :, None]
        mag = jnp.exp(lre[:, :, None, :] * nn)
        ang = lim[:, :, None, :] * nn
        return mag * jnp.cos(ang), mag * jnp.sin(ang)

    ab = (jnp.exp(lre) * jnp.cos(lim), jnp.exp(lre) * jnp.sin(lim))
    den = are * are + aim * aim
    quo = (((ab[0] - 1.0) * are + ab[1] * aim) / den, (ab[1] * are - (ab[0] - 1.0) * aim) / den)
    bbar = _cx_mul((quo[0][..., None], quo[1][..., None]), (b_re.astype(F32), b_im.astype(F32)))
    cmat = (c_re.astype(F32), c_im.astype(F32))

    pw = apow(jnp.arange(SUB + 1))
    cp = _cx_mul((cmat[0][:, :, None], cmat[1][:, :, None]),
                 (pw[0][:, :, :SUB, None, :], pw[1][:, :, :SUB, None, :]))
    kern = (jnp.einsum('dgkip,dgpj->dgkij', cp[0], bbar[0], precision=hi)
            - jnp.einsum('dgkip,dgpj->dgkij', cp[1], bbar[1], precision=hi))
    kf, kb = kern[0], kern[1]
    k16 = jnp.arange(SUB)
    rp, rr = k16[:, None], k16[None, :]
    idx = jnp.abs(rp - rr)
    kfe, kbe = kf[:, idx], kb[:, idx]
    up = (rp > rr)[None, :, :, None, None]
    lo = (rp < rr)[None, :, :, None, None]
    m = jnp.where(up, kfe, jnp.where(lo, kbe, kfe + kbe))
    mt = m.transpose(0, 1, 3, 2, 4).reshape(g, 256, 256)

    def cat(zf, zb, sign=1.0):
        return jnp.concatenate([zf[0], zb[0], sign * zf[1], sign * zb[1]], axis=-1)

    def sel(z, d, fn):
        return fn(z[0][d]), fn(z[1][d])

    bt = [(bbar[0][d].transpose(0, 2, 1)[:, None], bbar[1][d].transpose(0, 2, 1)[:, None])
          for d in range(2)]
    wf = _cx_mul(sel(pw, 0, lambda z: z[:, ::-1][:, 1:, None, :]), bt[0])
    wb = _cx_mul(sel(pw, 1, lambda z: z[:, :SUB, None, :]), bt[1])
    wbt = cat(wf, wb).reshape(g, 256, 256).transpose(0, 2, 1)
    cf = _cx_mul(sel(cmat, 0, lambda z: z[:, None]), sel(pw, 0, lambda z: z[:, 1:, None, :]))
    cb = _cx_mul(sel(cmat, 1, lambda z: z[:, None]),
                 sel(pw, 1, lambda z: z[:, ::-1][:, :SUB, None, :]))
    wct = cat(cf, cb, -1.0).reshape(g, 256, 256)

    def tab(nf, nb, mf, mb):
        zf = sel(apow(nf), 0, lambda z: z * mf[None, :, None])
        zb = sel(apow(nb), 1, lambda z: z * mb[None, :, None])
        return cat(zf, zb)

    m8 = jnp.arange(SUBLANES)
    nblk = Q_S5 // SUBLANES
    m16 = jnp.arange(nblk)
    parts = []
    for s in (1, 2, 4):
        n = jnp.full((SUBLANES,), SUB * s)
        parts.append(tab(n, n, (m8 >= s).astype(F32), (m8 < SUBLANES - s).astype(F32)))
    for s in (1, 2, 4, 8):
        n = jnp.full((nblk,), SUB * SUBLANES * s)
        parts.append(tab(n, n, (m16 >= s).astype(F32), (m16 < nblk - s).astype(F32)))
    one8 = jnp.ones((SUBLANES,), F32)
    parts.append(tab(SUB * (m8 + 1), SUB * (SUBLANES - m8), one8, one8))
    for n in (SUB, SUB * Q_S5):
        nn = jnp.full((SUBLANES,), n)
        parts.append(tab(nn, nn, one8, one8))
    table = jnp.concatenate(parts, axis=1)
    nq = SUB * jnp.arange(Q_S5)
    wq = tab(nq, nq, jnp.zeros((Q_S5,), F32), jnp.ones((Q_S5,), F32))
    return mt.astype(BF16), wbt.astype(BF16), wct.astype(BF16), table, wq


def _full(shape):
    return pl.BlockSpec(shape, lambda *_: (0,) * len(shape))


def _params(sem, **kw):
    return pltpu.CompilerParams(dimension_semantics=sem, vmem_limit_bytes=VMEM_LIMIT, **kw)


def _trunk(x, w):
    bsz, seq, _ = x.shape
    lq = seq // SUB
    assert seq % (SUB * Q_S5) == 0
    x4 = x.reshape(bsz, lq, SUB, D_MODEL)
    row_shape = jax.ShapeDtypeStruct((bsz, lq, SUB * D_MODEL), F32)

    nt = lq // Q_S5
    uz = pl.pallas_call(
        _in_t_kernel,
        grid=(bsz, nt, 2),
        in_specs=[pl.BlockSpec((1, Q_S5, SUB, D_MODEL), lambda b, t, c: (b, t, 0, 0)),
                  _full((1, D_MODEL)),
                  pl.BlockSpec((D_MODEL, D_MODEL), lambda b, t, c: (c, 0))],
        out_specs=pl.BlockSpec((1, SUB, D_MODEL, Q_S5), lambda b, t, c: (b, 0, c, t)),
        out_shape=jax.ShapeDtypeStruct((bsz, SUB, 2 * D_MODEL, lq), F32),
        scratch_shapes=[pltpu.VMEM((SUB * Q_S5, D_MODEL), BF16)],
        compiler_params=_params(("parallel", "parallel", "arbitrary")),
        name="in_proj_t",
    )(x4, w["norm_g"], w["w_s5_t"])

    rows = pl.BlockSpec((1, Q_IN, SUB * D_MODEL), lambda b, t: (b, t, 0))
    u_lru, z_lru, g_s5, g_lru = pl.pallas_call(
        _in_n_kernel,
        grid=(bsz, lq // Q_IN),
        in_specs=[pl.BlockSpec((1, Q_IN, SUB, D_MODEL), lambda b, t: (b, t, 0, 0)),
                  _full((1, D_MODEL)), _full((D_MODEL, 4 * D_MODEL))],
        out_specs=[rows] * 4,
        out_shape=[row_shape] * 4,
        compiler_params=_params(("parallel", "parallel")),
        name="in_proj_n",
    )(x4, w["norm_g"], w["w_rest"])

    def tile_of(ph, i):
        return jnp.where(ph == 0, nt - 1 - i, i)
    gw = pl.BlockSpec((GROUPS_PER_STEP, 256, 256), lambda g, b, ph, i: (g, 0, 0))
    y_t = pl.pallas_call(
        functools.partial(_s5_kernel, nt=nt),
        grid=(S5_GROUPS // GROUPS_PER_STEP, bsz, 2, nt),
        in_specs=[pl.BlockSpec((1, SUB, LANES, Q_S5), lambda g, b, ph, i: (b, 0, g, tile_of(ph, i))),
                  pl.BlockSpec((LANES, LANES), lambda g, b, ph, i: (g, 0)),
                  gw, gw, gw,
                  pl.BlockSpec((GROUPS_PER_STEP, T_ROWS, 2 * LANES), lambda g, b, ph, i: (g, 0, 0)),
                  pl.BlockSpec((GROUPS_PER_STEP, Q_S5, 2 * LANES), lambda g, b, ph, i: (g, 0, 0))],
        out_specs=pl.BlockSpec((1, SUB, LANES, Q_S5), lambda g, b, ph, i: (b, 0, g, ph * i)),
        out_shape=jax.ShapeDtypeStruct((bsz, SUB, D_MODEL, lq), F32),
        scratch_shapes=[pltpu.VMEM((2, Q_S5, 2 * LANES), F32),
                        pltpu.VMEM((GROUPS_PER_STEP, SUBLANES, 2 * LANES), F32),
                        pltpu.VMEM((GROUPS_PER_STEP, SUBLANES, 2 * LANES), F32),
                        pltpu.VMEM((nt, GROUPS_PER_STEP, SUBLANES, 2 * LANES), F32)],
        compiler_params=_params(("arbitrary",) * 4),
        name="s5_ssm",
    )(uz, w["s5_d_b"], w["s5_mt"], w["s5_wbt"], w["s5_wct"], w["s5_tab"], w["s5_wq"])

    m_s5 = pl.pallas_call(
        _s5_post_kernel,
        grid=(bsz, nt, SUB // R_BLK),
        in_specs=[pl.BlockSpec((1, R_BLK, D_MODEL, Q_S5), lambda b, t, r: (b, r, 0, t)),
                  pl.BlockSpec((1, R_BLK, D_MODEL, Q_S5), lambda b, t, r: (b, r, 1, t)),
                  pl.BlockSpec((1, Q_S5, R_BLK * D_MODEL), lambda b, t, r: (b, t, r)),
                  _full((D_MODEL, D_MODEL)), _full((D_MODEL, LANES)), _full((D_MODEL, D_MODEL))],
        out_specs=pl.BlockSpec((1, Q_S5, R_BLK * D_MODEL), lambda b, t, r: (b, t, r)),
        out_shape=row_shape,
        compiler_params=_params(("parallel", "parallel", "parallel")),
        name="s5_post",
    )(y_t, uz, g_s5, w["glu_wt"], w["glu_b_b"], w["s5_proj"])

    ntb = lq // Q_LRU_B
    hpb = Q_LRU_B // SUBLANES
    rows_b = pl.BlockSpec((1, Q_LRU_B, SUB * D_MODEL), lambda b, i: (b, ntb - 1 - i, 0))
    gate_w = _full((LRU_HEADS, LRU_BLOCK, LRU_BLOCK))
    gate_b = _full((LRU_HEADS, 1, LRU_BLOCK))
    xc, h_bwd = pl.pallas_call(
        functools.partial(_lru_bwd_kernel, nt=ntb),
        grid=(bsz, ntb),
        in_specs=[rows_b,
                  pl.BlockSpec((1, SUBLANES, SUB * D_MODEL),
                               lambda b, i: (b, jnp.maximum((ntb - 1 - i) * hpb - 1, 0), 0)),
                  _full((4, D_MODEL)), _full((1, D_MODEL)),
                  gate_w, gate_b, gate_w, gate_b, gate_b],
        out_specs=[rows_b, rows_b],
        out_shape=[row_shape, row_shape],
        scratch_shapes=[pltpu.VMEM((SUB, Q_LRU_B, LRU_BLOCK), F32),
                        pltpu.VMEM((SUBLANES, D_MODEL), F32),
                        pltpu.VMEM((SUBLANES, D_MODEL), F32)],
        compiler_params=_params(("arbitrary", "arbitrary")),
        name="lru_bwd",
    )(u_lru, u_lru, w["conv_w"], w["conv_b"], w["wa"][1], w["ba"][1], w["wx"][1], w["bx"][1],
      w["cvec"][1])

    rows_o = pl.BlockSpec((1, Q_OUT, SUB * D_MODEL), lambda b, i: (b, i, 0))
    x_o = pl.BlockSpec((1, Q_OUT, SUB, D_MODEL), lambda b, i: (b, i, 0, 0))
    out = pl.pallas_call(
        _out_kernel,
        grid=(bsz, lq // Q_OUT),
        in_specs=[x_o] + [rows_o] * 5 + [gate_w, gate_b, gate_w, gate_b, gate_b,
                                         _full((D_MODEL, D_MODEL)), _full((D_MODEL, D_MODEL)),
                                         _full((1, D_MODEL))],
        out_specs=x_o,
        out_shape=jax.ShapeDtypeStruct((bsz, lq, SUB, D_MODEL), F32),
        scratch_shapes=[pltpu.VMEM((1, Q_OUT, SUB * D_MODEL), F32),
                        pltpu.VMEM((SUB, Q_OUT, LRU_BLOCK), F32),
                        pltpu.VMEM((SUBLANES, D_MODEL), F32),
                        pltpu.VMEM((SUB * Q_OUT, D_MODEL), BF16)],
        compiler_params=_params(("arbitrary", "arbitrary")),
        name="lru_fwd_out",
    )(x4, xc, h_bwd, z_lru, g_lru, m_s5, w["wa"][0], w["ba"][0], w["wx"][0], w["bx"][0],
      w["cvec"][0], w["lru_proj"], w["w_out"], w["norm_f_g"])
    return out.reshape(bsz, seq, D_MODEL)


def kernel(x_prompt, x_sample, norm_g, w_in, s5_a_re, s5_a_im, s5_log_dt, s5_b_re, s5_b_im, s5_c_re, s5_c_im, s5_d, s5_glu_w, s5_glu_b, s5_proj, lru_conv_w, lru_conv_b, lru_lambda, lru_wa, lru_ba, lru_wx, lru_bx, lru_proj, w_out, norm_f_g):
    assert norm_g.shape[0] == 1, "single-layer trunk"
    wi = w_in[0]
    mt, wbt, wct, tab, wq = _prep_s5(s5_a_re[0], s5_a_im[0], s5_log_dt[0], s5_b_re[0], s5_b_im[0],
                                     s5_c_re[0], s5_c_im[0])
    w = {
        "norm_g": norm_g[0].reshape(1, D_MODEL).astype(F32),
        "w_s5_t": wi[:, :2 * D_MODEL].T.astype(BF16),
        "w_rest": wi[:, 2 * D_MODEL:].astype(BF16),
        "s5_mt": mt, "s5_wbt": wbt, "s5_wct": wct, "s5_tab": tab, "s5_wq": wq,
        "s5_d_b": jnp.broadcast_to(s5_d[0].astype(F32)[:, None], (D_MODEL, LANES)),
        "glu_wt": s5_glu_w[0].T.astype(BF16),
        "glu_b_b": jnp.broadcast_to(s5_glu_b[0].astype(F32)[:, None], (D_MODEL, LANES)),
        "s5_proj": s5_proj[0].astype(BF16),
        "conv_w": lru_conv_w[0].astype(F32),
        "conv_b": lru_conv_b[0].reshape(1, D_MODEL).astype(F32),
        "wa": lru_wa[0].astype(BF16),
        "wx": lru_wx[0].astype(BF16),
        "ba": lru_ba[0].reshape(2, LRU_HEADS, 1, LRU_BLOCK).astype(F32),
        "bx": lru_bx[0].reshape(2, LRU_HEADS, 1, LRU_BLOCK).astype(F32),
        "cvec": (-RG_C * jax.nn.softplus(-lru_lambda[0].astype(F32))).reshape(2, LRU_HEADS, 1, LRU_BLOCK),
        "lru_proj": lru_proj[0].astype(BF16),
        "w_out": w_out[0].astype(BF16),
        "norm_f_g": norm_f_g.reshape(1, D_MODEL).astype(F32),
    }
    return (_trunk(x_prompt, w), _trunk(x_sample, w))
```

```python
import functools

import jax
import jax.numpy as jnp
from jax import lax
from jax.experimental import pallas as pl
from jax.experimental.pallas import tpu as pltpu

F32 = jnp.float32
BF16 = jnp.bfloat16

D_MODEL = 1024
SUB = 16
PERM = SUB * SUB
S5_GROUPS = 64
S5_GROUP_SIZE = 16
S5_STATE = 64
GROUPS_PER_STEP = 8
LRU_HEADS = 8
LRU_BLOCK = 128
RG_C = 8.0
EPS = 1e-6
LANES = 128
SUBLANES = 8
Q_S5 = 128
R_BLK = 4
Q_IN = 16
Q_LRU_B = 64
Q_OUT = 32
VMEM_LIMIT = 56 * 1024 * 1024

T_STEP = 0
T_BLK = 24
T_FIX = 88
T_A16 = 96
T_AQ = 104
T_ROWS = 112


def _rms(x, g):
    return x * lax.rsqrt(jnp.mean(x * x, axis=-1, keepdims=True) + EPS) * g


def _cmul_cat(c, x):
    cre, cim = c[:, :LANES], c[:, LANES:]
    xre, xim = x[:, :LANES], x[:, LANES:]
    return jnp.concatenate([cre * xre - cim * xim, cre * xim + cim * xre], axis=1)


def _class_major(h, perm):
    return jnp.dot(perm, h, preferred_element_type=F32).astype(BF16)


def _in_t_kernel(x_ref, g_ref, perm_ref, w_ref, o_ref, h_ref):
    @pl.when(pl.program_id(2) == 0)
    def _():
        for gi in range(Q_S5 // SUB):
            xg = x_ref[0, gi * PERM:(gi + 1) * PERM, :]
            hp = _class_major(_rms(xg, g_ref[...]).astype(BF16), perm_ref[...])
            for r in range(SUB):
                h_ref[r * Q_S5 + gi * SUB:r * Q_S5 + (gi + 1) * SUB, :] = hp[r * SUB:(r + 1) * SUB, :]

    rows = R_BLK * Q_S5
    for rb in range(SUB // R_BLK):
        res = lax.dot_general(w_ref[...], h_ref[rb * rows:(rb + 1) * rows, :],
                              (((1,), (1,)), ((), ())), preferred_element_type=F32)
        for rr in range(R_BLK):
            o_ref[0, rb * R_BLK + rr] = res[:, rr * Q_S5:(rr + 1) * Q_S5]


def _in_n_kernel(x_ref, g_ref, perm_ref, w_ref, o0, o1, o2, o3):
    h = _class_major(_rms(x_ref[0], g_ref[...]).astype(BF16), perm_ref[...])
    for k, o in enumerate((o0, o1, o2, o3)):
        res = jnp.dot(h, w_ref[:, k * D_MODEL:(k + 1) * D_MODEL], preferred_element_type=F32)
        for r in range(SUB):
            o[0, :, r * D_MODEL:(r + 1) * D_MODEL] = res[r * Q_IN:(r + 1) * Q_IN, :]


def _s5_kernel(u_ref, d_ref, mt_ref, wbt_ref, wct_ref, tab_ref, wq_ref, o_ref,
               s_ref, cf_ref, cb_ref, hb_ref, *, nt):
    ph = pl.program_id(2)
    i = pl.program_id(3)
    tile = jnp.where(ph == 0, nt - 1 - i, i)
    q = Q_S5
    nblk = q // SUBLANES

    @pl.when(i == 0)
    def _():
        cf_ref[...] = jnp.zeros_like(cf_ref)
        cb_ref[...] = jnp.zeros_like(cb_ref)

    def lane_fwd(rows):
        lane = lax.broadcasted_iota(jnp.int32, (rows, 2 * LANES), 1)
        return (lane % LANES) < S5_STATE

    def load_x(gl):
        c0 = gl * S5_GROUP_SIZE
        xt = u_ref[0, :, c0:c0 + S5_GROUP_SIZE, :]
        return c0, xt, xt.reshape(SUB * S5_GROUP_SIZE, q).astype(BF16)

    def states(gl, xb):
        st = jnp.dot(wbt_ref[gl], xb, preferred_element_type=F32)
        return st.T

    @pl.when(ph == 0)
    def _():
        cbs = [cb_ref[gl] for gl in range(GROUPS_PER_STEP)]
        for gl in range(GROUPS_PER_STEP):
            _, _, xb = load_x(gl)
            s = states(gl, xb)
            hb_ref[tile, gl] = cbs[gl]
            ws = _cmul_cat(wq_ref[gl], s)
            red = jnp.sum(ws, axis=0, keepdims=True)
            cb_ref[gl] = (jnp.broadcast_to(red, (SUBLANES, 2 * LANES))
                          + _cmul_cat(tab_ref[gl, T_AQ:T_AQ + SUBLANES, :], cbs[gl]))

    @pl.when(ph == 1)
    def _():
        fwd8 = lane_fwd(SUBLANES)
        fwd16 = lane_fwd(nblk)
        row8 = lax.broadcasted_iota(jnp.int32, (SUBLANES, 2 * LANES), 0)
        row16 = lax.broadcasted_iota(jnp.int32, (nblk, 2 * LANES), 0)
        rowq = lax.broadcasted_iota(jnp.int32, (q, 2 * LANES), 0)
        fwdq = lane_fwd(q)

        cfs = [cf_ref[gl] for gl in range(GROUPS_PER_STEP)]
        cbs = [hb_ref[tile, gl] for gl in range(GROUPS_PER_STEP)]

        def group(gl, sc):
            c0, xt, xb = load_x(gl)
            sc[...] = states(gl, xb)
            cf_in = cfs[gl]
            cb_in = cbs[gl]
            a16 = tab_ref[gl, T_A16:T_A16 + SUBLANES, :]
            sc[0:SUBLANES, :] += jnp.where(fwd8 & (row8 == 0), _cmul_cat(a16, cf_in), 0.0)
            sc[q - SUBLANES:q, :] += jnp.where((~fwd8) & (row8 == SUBLANES - 1),
                                               _cmul_cat(a16, cb_in), 0.0)

            steps = [tab_ref[gl, T_STEP + 8 * si:T_STEP + 8 * si + 8, :] for si in range(3)]
            ends = []
            for k in range(nblk):
                b = sc[k * SUBLANES:(k + 1) * SUBLANES, :]
                for si, s in enumerate((1, 2, 4)):
                    sh = jnp.where(fwd8, pltpu.roll(b, s, 0), pltpu.roll(b, SUBLANES - s, 0))
                    b = b + _cmul_cat(steps[si], sh)
                sc[k * SUBLANES:(k + 1) * SUBLANES, :] = b
                ends.append(jnp.where(fwd8[0:1], b[SUBLANES - 1:SUBLANES, :], b[0:1, :]))

            e = jnp.concatenate(ends, axis=0)
            for si, s in enumerate((1, 2, 4, 8)):
                sh = jnp.where(fwd16, pltpu.roll(e, s, 0), pltpu.roll(e, nblk - s, 0))
                e = e + _cmul_cat(tab_ref[gl, T_BLK + 16 * si:T_BLK + 16 * si + 16, :], sh)
            eprev = jnp.where(fwd16,
                              jnp.where(row16 >= 1, pltpu.roll(e, 1, 0), 0.0),
                              jnp.where(row16 <= nblk - 2, pltpu.roll(e, nblk - 1, 0), 0.0))
            fix = tab_ref[gl, T_FIX:T_FIX + SUBLANES, :]
            for k in range(nblk):
                ev = jnp.broadcast_to(eprev[k:k + 1, :], (SUBLANES, 2 * LANES))
                sc[k * SUBLANES:(k + 1) * SUBLANES, :] += _cmul_cat(fix, ev)
            cf_ref[gl] = jnp.broadcast_to(e[nblk - 1:nblk, :], (SUBLANES, 2 * LANES))

            h = sc[...]
            dn = jnp.where(rowq == 0, jnp.broadcast_to(cf_in[0:1, :], (q, 2 * LANES)),
                           pltpu.roll(h, 1, 0))
            up = jnp.where(rowq == q - 1, jnp.broadcast_to(cb_in[0:1, :], (q, 2 * LANES)),
                           pltpu.roll(h, q - 1, 0))
            hprev = jnp.where(fwdq, dn, up).astype(BF16)
            y = jnp.dot(mt_ref[gl], xb, preferred_element_type=F32)
            y = y + lax.dot_general(wct_ref[gl], hprev, (((1,), (1,)), ((), ())),
                                    preferred_element_type=F32)
            dv = d_ref[c0:c0 + S5_GROUP_SIZE, :]
            o_ref[0, :, c0:c0 + S5_GROUP_SIZE, :] = (
                y.reshape(SUB, S5_GROUP_SIZE, q) + xt * dv[None])

        for gl in range(GROUPS_PER_STEP):
            group(gl, s_ref.at[gl])


def _s5_post_kernel(y_ref, z_ref, g_ref, wg_ref, bg_ref, wp_ref, o_ref):
    ys = [jax.nn.gelu(y_ref[0, rr]) for rr in range(R_BLK)]
    y = jnp.concatenate(ys, axis=1)
    glu = jnp.dot(wg_ref[...], y.astype(BF16), preferred_element_type=F32)
    bias = jnp.concatenate([bg_ref[...]] * R_BLK, axis=1)
    y = y * jax.nn.sigmoid(glu + bias)
    z = jnp.concatenate([z_ref[0, rr] for rr in range(R_BLK)], axis=1)
    y = (y * jax.nn.silu(z)).astype(BF16)
    ys5 = lax.dot_general(y, wp_ref[...], (((0,), (0,)), ((), ())),
                          preferred_element_type=F32)
    for rr in range(R_BLK):
        g = g_ref[0, :, rr * D_MODEL:(rr + 1) * D_MODEL]
        o_ref[0, :, rr * D_MODEL:(rr + 1) * D_MODEL] = (
            jax.nn.sigmoid(g) * ys5[rr * Q_S5:(rr + 1) * Q_S5, :])


def _lru_gates(xc_ref, hd, wa_ref, ba_ref, wx_ref, bx_ref, cv_ref, nq):
    c0 = pl.multiple_of(hd * LRU_BLOCK, LRU_BLOCK)
    xs = [xc_ref[0, :, pl.ds(pl.multiple_of(r * D_MODEL + c0, LRU_BLOCK), LRU_BLOCK)]
          for r in range(SUB)]
    xc = jnp.concatenate(xs, axis=0)
    xb = xc.astype(BF16)
    ra = jnp.dot(xb, wa_ref[hd], preferred_element_type=F32) + ba_ref[hd]
    rx = jnp.dot(xb, wx_ref[hd], preferred_element_type=F32) + bx_ref[hd]
    log_a = cv_ref[hd] * jax.nn.sigmoid(ra)
    a = jnp.exp(log_a)
    b = jnp.sqrt(1.0 - a * a) * (jax.nn.sigmoid(rx) * xc)
    return a, b


def _lru_scan(a, b, h_ref, col_of, p_ref, carry_row, nq, reverse):
    order = list(range(SUB - 1, -1, -1)) if reverse else list(range(SUB))
    h = None
    p = None
    for r in order:
        ar = a[r * nq:(r + 1) * nq, :]
        br = b[r * nq:(r + 1) * nq, :]
        h = br if h is None else ar * h + br
        p = ar if p is None else ar * p
        h_ref[0, :, col_of(r)] = h
        p_ref[r] = p

    row = lax.broadcasted_iota(jnp.int32, (nq, LRU_BLOCK), 0)
    s = 1
    while s < nq:
        keep = (row < nq - s) if reverse else (row >= s)
        shift = (nq - s) if reverse else s
        hs = jnp.where(keep, pltpu.roll(h, shift, 0), 0.0)
        ps = jnp.where(keep, pltpu.roll(p, shift, 0), 1.0)
        h = h + p * hs
        p = p * ps
        s *= 2
    e = h + p * carry_row
    if reverse:
        cin = jnp.where(row < nq - 1, pltpu.roll(e, nq - 1, 0), carry_row)
        carry_out = e[0:1, :]
    else:
        cin = jnp.where(row >= 1, pltpu.roll(e, 1, 0), carry_row)
        carry_out = e[nq - 1:nq, :]
    for r in order:
        h_ref[0, :, col_of(r)] = h_ref[0, :, col_of(r)] + p_ref[r] * cin
    return carry_out


def _lru_bwd_kernel(u_ref, halo_ref, cw_ref, cbias_ref, wa_ref, ba_ref, wx_ref, bx_ref, cv_ref,
                    xc_ref, h_ref, p_ref, nx_ref, carry_ref, *, nt):
    i = pl.program_id(1)
    tile = nt - 1 - i
    nq = Q_LRU_B

    @pl.when(i == 0)
    def _():
        nx_ref[...] = jnp.zeros_like(nx_ref)
        carry_ref[...] = jnp.zeros_like(carry_ref)

    def cls(r):
        return u_ref[0, :, r * D_MODEL:(r + 1) * D_MODEL]

    row = lax.broadcasted_iota(jnp.int32, (nq, D_MODEL), 0)
    live = (tile > 0).astype(F32)
    hp = halo_ref[0, SUBLANES - 1:SUBLANES, :] * live
    d14 = jnp.where(row == 0, hp[:, 14 * D_MODEL:15 * D_MODEL], pltpu.roll(cls(14), 1, 0))
    d15 = jnp.where(row == 0, hp[:, 15 * D_MODEL:16 * D_MODEL], pltpu.roll(cls(15), 1, 0))
    u0 = jnp.where(row == nq - 1, nx_ref[0:1, :], pltpu.roll(cls(0), nq - 1, 0))
    nx_new = u_ref[0, 0:1, 0:D_MODEL]

    def tap(r):
        if r == -2:
            return d14
        if r == -1:
            return d15
        if r == SUB:
            return u0
        return cls(r)

    w = cw_ref[...]
    for r in range(SUB):
        xc = (cbias_ref[...] + tap(r - 2) * w[0:1] + tap(r - 1) * w[1:2]
              + tap(r) * w[2:3] + tap(r + 1) * w[3:4])
        xc_ref[0, :, r * D_MODEL:(r + 1) * D_MODEL] = xc
    nx_ref[0:1, :] = nx_new

    def head(hd, c):
        c0 = pl.multiple_of(hd * LRU_BLOCK, LRU_BLOCK)
        a, b = _lru_gates(xc_ref, hd, wa_ref, ba_ref, wx_ref, bx_ref, cv_ref, nq)

        def col_of(r):
            return pl.ds(pl.multiple_of(r * D_MODEL + c0, LRU_BLOCK), LRU_BLOCK)
        cout = _lru_scan(a, b, h_ref, col_of, p_ref,
                         carry_ref[0:1, pl.ds(c0, LRU_BLOCK)], nq, True)
        carry_ref[0:1, pl.ds(c0, LRU_BLOCK)] = cout
        return c
    lax.fori_loop(0, LRU_HEADS, head, 0)


def _out_kernel(x_ref, xc_ref, hb_ref, z_ref, g_ref, m_ref, wa_ref, ba_ref, wx_ref, bx_ref, cv_ref,
                wl_ref, wo_ref, gf_ref, perm_ref, o_ref, hf_ref, p_ref, carry_ref, v_ref):
    i = pl.program_id(1)
    nq = Q_OUT

    @pl.when(i == 0)
    def _():
        carry_ref[...] = jnp.zeros_like(carry_ref)

    def head(hd, c):
        c0 = pl.multiple_of(hd * LRU_BLOCK, LRU_BLOCK)
        a, b = _lru_gates(xc_ref, hd, wa_ref, ba_ref, wx_ref, bx_ref, cv_ref, nq)

        def col_of(r):
            return pl.ds(pl.multiple_of(r * D_MODEL + c0, LRU_BLOCK), LRU_BLOCK)
        cout = _lru_scan(a, b, hf_ref, col_of, p_ref,
                         carry_ref[0:1, pl.ds(c0, LRU_BLOCK)], nq, False)
        carry_ref[0:1, pl.ds(c0, LRU_BLOCK)] = cout
        for r in range(SUB):
            hl = hf_ref[0, :, col_of(r)] + hb_ref[0, :, col_of(r)]
            v = hl * jax.nn.silu(z_ref[0, :, col_of(r)])
            v_ref[r * nq:(r + 1) * nq, pl.ds(c0, LRU_BLOCK)] = v.astype(BF16)
        return c
    lax.fori_loop(0, LRU_HEADS, head, 0)

    ylru = jnp.dot(v_ref[...], wl_ref[...], preferred_element_type=F32)
    ms = []
    for r in range(SUB):
        sl = slice(r * D_MODEL, (r + 1) * D_MODEL)
        ms.append((m_ref[0, :, sl] + jax.nn.sigmoid(g_ref[0, :, sl]) * ylru[r * nq:(r + 1) * nq, :])
                  .astype(BF16))
    for gi in range(nq // SUB):
        mg = jnp.concatenate([m[gi * SUB:(gi + 1) * SUB, :] for m in ms], axis=0)
        mn = _class_major(mg, perm_ref[...])
        dm = jnp.dot(mn, wo_ref[...], preferred_element_type=F32)
        xo = x_ref[0, gi * PERM:(gi + 1) * PERM, :] + dm
        o_ref[0, gi * PERM:(gi + 1) * PERM, :] = _rms(xo, gf_ref[...])


def _cx_mul(a, b):
    return a[0] * b[0] - a[1] * b[1], a[0] * b[1] + a[1] * b[0]


def _prep_s5(a_re, a_im, log_dt, b_re, b_im, c_re, c_im):
    hi = lax.Precision.HIGHEST
    g = S5_GROUPS
    are, aim = a_re.astype(F32), a_im.astype(F32)
    dt = jnp.exp(log_dt.astype(F32))[..., None]
    lre, lim = are * dt, aim * dt

    def apow(n):
        nn = n.astype(F32)[None, None, :, None]
        mag = jnp.exp(lre[:, :, None, :] * nn)
        ang = lim[:, :, None, :] * nn
        return mag * jnp.cos(ang), mag * jnp.sin(ang)

    ab = (jnp.exp(lre) * jnp.cos(lim), jnp.exp(lre) * jnp.sin(lim))
    den = are * are + aim * aim
    quo = (((ab[0] - 1.0) * are + ab[1] * aim) / den, (ab[1] * are - (ab[0] - 1.0) * aim) / den)
    bbar = _cx_mul((quo[0][..., None], quo[1][..., None]), (b_re.astype(F32), b_im.astype(F32)))
    cmat = (c_re.astype(F32), c_im.astype(F32))

    pw = apow(jnp.arange(SUB + 1))
    cp = _cx_mul((cmat[0][:, :, None], cmat[1][:, :, None]),
                 (pw[0][:, :, :SUB, None, :], pw[1][:, :, :SUB, None, :]))
    kern = (jnp.einsum('dgkip,dgpj->dgkij', cp[0], bbar[0], precision=hi)
            - jnp.einsum('dgkip,dgpj->dgkij', cp[1], bbar[1], precision=hi))
    kf, kb = kern[0], kern[1]
    k16 = jnp.arange(SUB)
    rp, rr = k16[:, None], k16[None, :]
    idx = jnp.abs(rp - rr)
    kfe, kbe = kf[:, idx], kb[:, idx]
    up = (rp > rr)[None, :, :, None, None]
    lo = (rp < rr)[None, :, :, None, None]
    m = jnp.where(up, kfe, jnp.where(lo, kbe, kfe + kbe))
    mt = m.transpose(0, 1, 3, 2, 4).reshape(g, 256, 256)

    def cat(zf, zb, sign=1.0):
        return jnp.concatenate([zf[0], zb[0], sign * zf[1], sign * zb[1]], axis=-1)

    def sel(z, d, fn):
        return fn(z[0][d]), fn(z[1][d])

    bt = [(bbar[0][d].transpose(0, 2, 1)[:, None], bbar[1][d].transpose(0, 2, 1)[:, None])
          for d in range(2)]
    wf = _cx_mul(sel(pw, 0, lambda z: z[:, ::-1][:, 1:, None, :]), bt[0])
    wb = _cx_mul(sel(pw, 1, lambda z: z[:, :SUB, None, :]), bt[1])
    wbt = cat(wf, wb).reshape(g, 256, 256).transpose(0, 2, 1)
    cf = _cx_mul(sel(cmat, 0, lambda z: z[:, None]), sel(pw, 0, lambda z: z[:, 1:, None, :]))
    cb = _cx_mul(sel(cmat, 1, lambda z: z[:, None]),
                 sel(pw, 1, lambda z: z[:, ::-1][:, :SUB, None, :]))
    wct = cat(cf, cb, -1.0).reshape(g, 256, 256)

    def tab(nf, nb, mf, mb):
        zf = sel(apow(nf), 0, lambda z: z * mf[None, :, None])
        zb = sel(apow(nb), 1, lambda z: z * mb[None, :, None])
        return cat(zf, zb)

    m8 = jnp.arange(SUBLANES)
    nblk = Q_S5 // SUBLANES
    m16 = jnp.arange(nblk)
    parts = []
    for s in (1, 2, 4):
        n = jnp.full((SUBLANES,), SUB * s)
        parts.append(tab(n, n, (m8 >= s).astype(F32), (m8 < SUBLANES - s).astype(F32)))
    for s in (1, 2, 4, 8):
        n = jnp.full((nblk,), SUB * SUBLANES * s)
        parts.append(tab(n, n, (m16 >= s).astype(F32), (m16 < nblk - s).astype(F32)))
    one8 = jnp.ones((SUBLANES,), F32)
    parts.append(tab(SUB * (m8 + 1), SUB * (SUBLANES - m8), one8, one8))
    for n in (SUB, SUB * Q_S5):
        nn = jnp.full((SUBLANES,), n)
        parts.append(tab(nn, nn, one8, one8))
    table = jnp.concatenate(parts, axis=1)
    nq = SUB * jnp.arange(Q_S5)
    wq = tab(nq, nq, jnp.zeros((Q_S5,), F32), jnp.ones((Q_S5,), F32))
    return mt.astype(BF16), wbt.astype(BF16), wct.astype(BF16), table, wq


def _full(shape):
    return pl.BlockSpec(shape, lambda *_: (0,) * len(shape))


def _params(sem, **kw):
    return pltpu.CompilerParams(dimension_semantics=sem, vmem_limit_bytes=VMEM_LIMIT, **kw)


def _trunk(x, w):
    bsz, seq, _ = x.shape
    lq = seq // SUB
    assert seq % (SUB * Q_S5) == 0
    row_shape = jax.ShapeDtypeStruct((bsz, lq, SUB * D_MODEL), F32)

    nt = lq // Q_S5
    uz = pl.pallas_call(
        _in_t_kernel,
        grid=(bsz, nt, 2),
        in_specs=[pl.BlockSpec((1, SUB * Q_S5, D_MODEL), lambda b, t, c: (b, t, 0)),
                  _full((1, D_MODEL)), _full((PERM, PERM)),
                  pl.BlockSpec((D_MODEL, D_MODEL), lambda b, t, c: (c, 0))],
        out_specs=pl.BlockSpec((1, SUB, D_MODEL, Q_S5), lambda b, t, c: (b, 0, c, t)),
        out_shape=jax.ShapeDtypeStruct((bsz, SUB, 2 * D_MODEL, lq), F32),
        scratch_shapes=[pltpu.VMEM((SUB * Q_S5, D_MODEL), BF16)],
        compiler_params=_params(("parallel", "parallel", "arbitrary")),
        name="in_proj_t",
    )(x, w["norm_g"], w["perm"], w["w_s5_t"])

    rows = pl.BlockSpec((1, Q_IN, SUB * D_MODEL), lambda b, t: (b, t, 0))
    u_lru, z_lru, g_s5, g_lru = pl.pallas_call(
        _in_n_kernel,
        grid=(bsz, lq // Q_IN),
        in_specs=[pl.BlockSpec((1, PERM, D_MODEL), lambda b, t: (b, t, 0)),
                  _full((1, D_MODEL)), _full((PERM, PERM)), _full((D_MODEL, 4 * D_MODEL))],
        out_specs=[rows] * 4,
        out_shape=[row_shape] * 4,
        compiler_params=_params(("parallel", "parallel")),
        name="in_proj_n",
    )(x, w["norm_g"], w["perm"], w["w_rest"])

    def tile_of(ph, i):
        return jnp.where(ph == 0, nt - 1 - i, i)
    gw = pl.BlockSpec((GROUPS_PER_STEP, 256, 256), lambda g, b, ph, i: (g, 0, 0))
    y_t = pl.pallas_call(
        functools.partial(_s5_kernel, nt=nt),
        grid=(S5_GROUPS // GROUPS_PER_STEP, bsz, 2, nt),
        in_specs=[pl.BlockSpec((1, SUB, LANES, Q_S5), lambda g, b, ph, i: (b, 0, g, tile_of(ph, i))),
                  pl.BlockSpec((LANES, LANES), lambda g, b, ph, i: (g, 0)),
                  gw, gw, gw,
                  pl.BlockSpec((GROUPS_PER_STEP, T_ROWS, 2 * LANES), lambda g, b, ph, i: (g, 0, 0)),
                  pl.BlockSpec((GROUPS_PER_STEP, Q_S5, 2 * LANES), lambda g, b, ph, i: (g, 0, 0))],
        out_specs=pl.BlockSpec((1, SUB, LANES, Q_S5), lambda g, b, ph, i: (b, 0, g, ph * i)),
        out_shape=jax.ShapeDtypeStruct((bsz, SUB, D_MODEL, lq), F32),
        scratch_shapes=[pltpu.VMEM((GROUPS_PER_STEP, Q_S5, 2 * LANES), F32),
                        pltpu.VMEM((GROUPS_PER_STEP, SUBLANES, 2 * LANES), F32),
                        pltpu.VMEM((GROUPS_PER_STEP, SUBLANES, 2 * LANES), F32),
                        pltpu.VMEM((nt, GROUPS_PER_STEP, SUBLANES, 2 * LANES), F32)],
        compiler_params=_params(("arbitrary",) * 4),
        name="s5_ssm",
    )(uz, w["s5_d_b"], w["s5_mt"], w["s5_wbt"], w["s5_wct"], w["s5_tab"], w["s5_wq"])

    m_s5 = pl.pallas_call(
        _s5_post_kernel,
        grid=(bsz, nt, SUB // R_BLK),
        in_specs=[pl.BlockSpec((1, R_BLK, D_MODEL, Q_S5), lambda b, t, r: (b, r, 0, t)),
                  pl.BlockSpec((1, R_BLK, D_MODEL, Q_S5), lambda b, t, r: (b, r, 1, t)),
                  pl.BlockSpec((1, Q_S5, R_BLK * D_MODEL), lambda b, t, r: (b, t, r)),
                  _full((D_MODEL, D_MODEL)), _full((D_MODEL, LANES)), _full((D_MODEL, D_MODEL))],
        out_specs=pl.BlockSpec((1, Q_S5, R_BLK * D_MODEL), lambda b, t, r: (b, t, r)),
        out_shape=row_shape,
        compiler_params=_params(("parallel", "parallel", "parallel")),
        name="s5_post",
    )(y_t, uz, g_s5, w["glu_wt"], w["glu_b_b"], w["s5_proj"])

    ntb = lq // Q_LRU_B
    hpb = Q_LRU_B // SUBLANES
    rows_b = pl.BlockSpec((1, Q_LRU_B, SUB * D_MODEL), lambda b, i: (b, ntb - 1 - i, 0))
    gate_w = _full((LRU_HEADS, LRU_BLOCK, LRU_BLOCK))
    gate_b = _full((LRU_HEADS, 1, LRU_BLOCK))
    xc, h_bwd = pl.pallas_call(
        functools.partial(_lru_bwd_kernel, nt=ntb),
        grid=(bsz, ntb),
        in_specs=[rows_b,
                  pl.BlockSpec((1, SUBLANES, SUB * D_MODEL),
                               lambda b, i: (b, jnp.maximum((ntb - 1 - i) * hpb - 1, 0), 0)),
                  _full((4, D_MODEL)), _full((1, D_MODEL)),
                  gate_w, gate_b, gate_w, gate_b, gate_b],
        out_specs=[rows_b, rows_b],
        out_shape=[row_shape, row_shape],
        scratch_shapes=[pltpu.VMEM((SUB, Q_LRU_B, LRU_BLOCK), F32),
                        pltpu.VMEM((SUBLANES, D_MODEL), F32),
                        pltpu.VMEM((SUBLANES, D_MODEL), F32)],
        compiler_params=_params(("arbitrary", "arbitrary")),
        name="lru_bwd",
    )(u_lru, u_lru, w["conv_w"], w["conv_b"], w["wa"][1], w["ba"][1], w["wx"][1], w["bx"][1],
      w["cvec"][1])

    rows_o = pl.BlockSpec((1, Q_OUT, SUB * D_MODEL), lambda b, i: (b, i, 0))
    x_o = pl.BlockSpec((1, SUB * Q_OUT, D_MODEL), lambda b, i: (b, i, 0))
    out = pl.pallas_call(
        _out_kernel,
        grid=(bsz, lq // Q_OUT),
        in_specs=[x_o] + [rows_o] * 5 + [gate_w, gate_b, gate_w, gate_b, gate_b,
                                         _full((D_MODEL, D_MODEL)), _full((D_MODEL, D_MODEL)),
                                         _full((1, D_MODEL)), _full((PERM, PERM))],
        out_specs=x_o,
        out_shape=jax.ShapeDtypeStruct((bsz, seq, D_MODEL), F32),
        scratch_shapes=[pltpu.VMEM((1, Q_OUT, SUB * D_MODEL), F32),
                        pltpu.VMEM((SUB, Q_OUT, LRU_BLOCK), F32),
                        pltpu.VMEM((SUBLANES, D_MODEL), F32),
                        pltpu.VMEM((SUB * Q_OUT, D_MODEL), BF16)],
        compiler_params=_params(("arbitrary", "arbitrary")),
        name="lru_fwd_out",
    )(x, xc, h_bwd, z_lru, g_lru, m_s5, w["wa"][0], w["ba"][0], w["wx"][0], w["bx"][0],
      w["cvec"][0], w["lru_proj"], w["w_out"], w["norm_f_g"], w["perm"])
    return out


def kernel(x_prompt, x_sample, norm_g, w_in, s5_a_re, s5_a_im, s5_log_dt, s5_b_re, s5_b_im, s5_c_re, s5_c_im, s5_d, s5_glu_w, s5_glu_b, s5_proj, lru_conv_w, lru_conv_b, lru_lambda, lru_wa, lru_ba, lru_wx, lru_bx, lru_proj, w_out, norm_f_g):
    assert norm_g.shape[0] == 1, "single-layer trunk"
    wi = w_in[0]
    mt, wbt, wct, tab, wq = _prep_s5(s5_a_re[0], s5_a_im[0], s5_log_dt[0], s5_b_re[0], s5_b_im[0],
                                     s5_c_re[0], s5_c_im[0])
    tok = jnp.arange(PERM)
    w = {
        "perm": (tok[None, :] == SUB * (tok % SUB)[:, None] + (tok // SUB)[:, None]).astype(BF16),
        "norm_g": norm_g[0].reshape(1, D_MODEL).astype(F32),
        "w_s5_t": wi[:, :2 * D_MODEL].T.astype(BF16),
        "w_rest": wi[:, 2 * D_MODEL:].astype(BF16),
        "s5_mt": mt, "s5_wbt": wbt, "s5_wct": wct, "s5_tab": tab, "s5_wq": wq,
        "s5_d_b": jnp.broadcast_to(s5_d[0].astype(F32)[:, None], (D_MODEL, LANES)),
        "glu_wt": s5_glu_w[0].T.astype(BF16),
        "glu_b_b": jnp.broadcast_to(s5_glu_b[0].astype(F32)[:, None], (D_MODEL, LANES)),
        "s5_proj": s5_proj[0].astype(BF16),
        "conv_w": lru_conv_w[0].astype(F32),
        "conv_b": lru_conv_b[0].reshape(1, D_MODEL).astype(F32),
        "wa": lru_wa[0].astype(BF16),
        "wx": lru_wx[0].astype(BF16),
        "ba": lru_ba[0].reshape(2, LRU_HEADS, 1, LRU_BLOCK).astype(F32),
        "bx": lru_bx[0].reshape(2, LRU_HEADS, 1, LRU_BLOCK).astype(F32),
        "cvec": (-RG_C * jax.nn.softplus(-lru_lambda[0].astype(F32))).reshape(2, LRU_HEADS, 1, LRU_BLOCK),
        "lru_proj": lru_proj[0].astype(BF16),
        "w_out": w_out[0].astype(BF16),
        "norm_f_g": norm_f_g.reshape(1, D_MODEL).astype(F32),
    }
    return (_trunk(x_prompt, w), _trunk(x_sample, w))
```

```python
import functools

import jax
import jax.numpy as jnp
from jax import lax
from jax.experimental import pallas as pl
from jax.experimental.pallas import tpu as pltpu

F32 = jnp.float32
BF16 = jnp.bfloat16

D_MODEL = 1024
SUB = 16
PERM = SUB * SUB
S5_GROUPS = 64
S5_GROUP_SIZE = 16
S5_STATE = 64
GROUPS_PER_STEP = 8
LRU_HEADS = 8
LRU_BLOCK = 128
LRU_PAIR = 2 * LRU_BLOCK
RG_C = 8.0
EPS = 1e-6
LANES = 128
SUBLANES = 8
Q_S5 = 128
R_BLK = 4
Q_IN = 16
Q_LRU_B = 64
Q_OUT = 32
VMEM_LIMIT = 56 * 1024 * 1024

T_STEP = 0
T_BLK = 24
T_FIX = 88
T_A16 = 96
T_AQ = 104
T_ROWS = 112


def _rms(x, g):
    return x * lax.rsqrt(jnp.mean(x * x, axis=-1, keepdims=True) + EPS) * g


def _cmul_cat(c, x):
    cre, cim = c[:, :LANES], c[:, LANES:]
    xre, xim = x[:, :LANES], x[:, LANES:]
    return jnp.concatenate([cre * xre - cim * xim, cre * xim + cim * xre], axis=1)


def _class_major(h, perm):
    return jnp.dot(perm, h, preferred_element_type=F32).astype(BF16)


def _in_t_kernel(x_ref, g_ref, perm_ref, w_ref, o_ref, h_ref):
    @pl.when(pl.program_id(2) == 0)
    def _():
        for gi in range(Q_S5 // SUB):
            xg = x_ref[0, gi * PERM:(gi + 1) * PERM, :]
            hp = _class_major(_rms(xg, g_ref[...]).astype(BF16), perm_ref[...])
            for r in range(SUB):
                h_ref[r * Q_S5 + gi * SUB:r * Q_S5 + (gi + 1) * SUB, :] = hp[r * SUB:(r + 1) * SUB, :]

    rows = R_BLK * Q_S5
    for rb in range(SUB // R_BLK):
        res = lax.dot_general(w_ref[...], h_ref[rb * rows:(rb + 1) * rows, :],
                              (((1,), (1,)), ((), ())), preferred_element_type=F32)
        for rr in range(R_BLK):
            o_ref[0, rb * R_BLK + rr] = res[:, rr * Q_S5:(rr + 1) * Q_S5]


def _in_n_kernel(x_ref, g_ref, perm_ref, w_ref, o0, o1, o2, o3):
    h = _class_major(_rms(x_ref[0], g_ref[...]).astype(BF16), perm_ref[...])
    for k, o in enumerate((o0, o1, o2, o3)):
        res = jnp.dot(h, w_ref[:, k * D_MODEL:(k + 1) * D_MODEL], preferred_element_type=F32)
        for r in range(SUB):
            o[0, :, r * D_MODEL:(r + 1) * D_MODEL] = res[r * Q_IN:(r + 1) * Q_IN, :]


def _s5_kernel(u_ref, d_ref, mt_ref, wbt_ref, wct_ref, tab_ref, wq_ref, o_ref,
               s_ref, cf_ref, cb_ref, hb_ref, *, nt):
    ph = pl.program_id(2)
    i = pl.program_id(3)
    tile = jnp.where(ph == 0, nt - 1 - i, i)
    q = Q_S5
    nblk = q // SUBLANES

    @pl.when(i == 0)
    def _():
        cf_ref[...] = jnp.zeros_like(cf_ref)
        cb_ref[...] = jnp.zeros_like(cb_ref)

    def lane_fwd(rows):
        lane = lax.broadcasted_iota(jnp.int32, (rows, 2 * LANES), 1)
        return (lane % LANES) < S5_STATE

    def load_x(gl):
        c0 = gl * S5_GROUP_SIZE
        xt = u_ref[0, :, c0:c0 + S5_GROUP_SIZE, :]
        return c0, xt, xt.reshape(SUB * S5_GROUP_SIZE, q).astype(BF16)

    def states(gl, xb):
        st = jnp.dot(wbt_ref[gl], xb, preferred_element_type=F32)
        return st.T

    @pl.when(ph == 0)
    def _():
        cbs = [cb_ref[gl] for gl in range(GROUPS_PER_STEP)]
        for gl in range(GROUPS_PER_STEP):
            _, _, xb = load_x(gl)
            s = states(gl, xb)
            hb_ref[tile, gl] = cbs[gl]
            ws = _cmul_cat(wq_ref[gl], s)
            red = jnp.sum(ws, axis=0, keepdims=True)
            cb_ref[gl] = (jnp.broadcast_to(red, (SUBLANES, 2 * LANES))
                          + _cmul_cat(tab_ref[gl, T_AQ:T_AQ + SUBLANES, :], cbs[gl]))

    @pl.when(ph == 1)
    def _():
        fwd8 = lane_fwd(SUBLANES)
        fwd16 = lane_fwd(nblk)
        row8 = lax.broadcasted_iota(jnp.int32, (SUBLANES, 2 * LANES), 0)
        row16 = lax.broadcasted_iota(jnp.int32, (nblk, 2 * LANES), 0)
        rowq = lax.broadcasted_iota(jnp.int32, (q, 2 * LANES), 0)
        fwdq = lane_fwd(q)

        cfs = [cf_ref[gl] for gl in range(GROUPS_PER_STEP)]
        cbs = [hb_ref[tile, gl] for gl in range(GROUPS_PER_STEP)]

        def stage_states(gl):
            c0, xt, xb = load_x(gl)
            sc = s_ref.at[gl]
            sc[...] = states(gl, xb)
            y = jnp.dot(mt_ref[gl], xb, preferred_element_type=F32)
            dv = d_ref[c0:c0 + S5_GROUP_SIZE, :]
            o_ref[0, :, c0:c0 + S5_GROUP_SIZE, :] = (
                y.reshape(SUB, S5_GROUP_SIZE, q) + xt * dv[None])

        def stage_scan(gl):
            sc = s_ref.at[gl]
            cf_in = cfs[gl]
            cb_in = cbs[gl]
            a16 = tab_ref[gl, T_A16:T_A16 + SUBLANES, :]
            sc[0:SUBLANES, :] += jnp.where(fwd8 & (row8 == 0), _cmul_cat(a16, cf_in), 0.0)
            sc[q - SUBLANES:q, :] += jnp.where((~fwd8) & (row8 == SUBLANES - 1),
                                               _cmul_cat(a16, cb_in), 0.0)

            steps = [tab_ref[gl, T_STEP + 8 * si:T_STEP + 8 * si + 8, :] for si in range(3)]
            ends = []
            for k in range(nblk):
                b = sc[k * SUBLANES:(k + 1) * SUBLANES, :]
                for si, s in enumerate((1, 2, 4)):
                    sh = jnp.where(fwd8, pltpu.roll(b, s, 0), pltpu.roll(b, SUBLANES - s, 0))
                    b = b + _cmul_cat(steps[si], sh)
                sc[k * SUBLANES:(k + 1) * SUBLANES, :] = b
                ends.append(jnp.where(fwd8[0:1], b[SUBLANES - 1:SUBLANES, :], b[0:1, :]))

            e = jnp.concatenate(ends, axis=0)
            for si, s in enumerate((1, 2, 4, 8)):
                sh = jnp.where(fwd16, pltpu.roll(e, s, 0), pltpu.roll(e, nblk - s, 0))
                e = e + _cmul_cat(tab_ref[gl, T_BLK + 16 * si:T_BLK + 16 * si + 16, :], sh)
            eprev = jnp.where(fwd16,
                              jnp.where(row16 >= 1, pltpu.roll(e, 1, 0), 0.0),
                              jnp.where(row16 <= nblk - 2, pltpu.roll(e, nblk - 1, 0), 0.0))
            fix = tab_ref[gl, T_FIX:T_FIX + SUBLANES, :]
            for k in range(nblk):
                ev = jnp.broadcast_to(eprev[k:k + 1, :], (SUBLANES, 2 * LANES))
                sc[k * SUBLANES:(k + 1) * SUBLANES, :] += _cmul_cat(fix, ev)
            cf_ref[gl] = jnp.broadcast_to(e[nblk - 1:nblk, :], (SUBLANES, 2 * LANES))

        def stage_out(gl):
            c0 = gl * S5_GROUP_SIZE
            h = s_ref[gl]
            dn = jnp.where(rowq == 0, jnp.broadcast_to(cfs[gl][0:1, :], (q, 2 * LANES)),
                           pltpu.roll(h, 1, 0))
            up = jnp.where(rowq == q - 1, jnp.broadcast_to(cbs[gl][0:1, :], (q, 2 * LANES)),
                           pltpu.roll(h, q - 1, 0))
            hprev = jnp.where(fwdq, dn, up).astype(BF16)
            y = lax.dot_general(wct_ref[gl], hprev, (((1,), (1,)), ((), ())),
                                preferred_element_type=F32)
            o_ref[0, :, c0:c0 + S5_GROUP_SIZE, :] += y.reshape(SUB, S5_GROUP_SIZE, q)

        for step in range(GROUPS_PER_STEP + 2):
            if step < GROUPS_PER_STEP:
                stage_states(step)
            if 0 <= step - 1 < GROUPS_PER_STEP:
                stage_scan(step - 1)
            if 0 <= step - 2 < GROUPS_PER_STEP:
                stage_out(step - 2)


def _s5_post_kernel(y_ref, z_ref, g_ref, wg_ref, bg_ref, wp_ref, o_ref):
    ys = [jax.nn.gelu(y_ref[0, rr]) for rr in range(R_BLK)]
    y = jnp.concatenate(ys, axis=1)
    glu = jnp.dot(wg_ref[...], y.astype(BF16), preferred_element_type=F32)
    bias = jnp.concatenate([bg_ref[...]] * R_BLK, axis=1)
    y = y * jax.nn.sigmoid(glu + bias)
    z = jnp.concatenate([z_ref[0, rr] for rr in range(R_BLK)], axis=1)
    y = (y * jax.nn.silu(z)).astype(BF16)
    ys5 = lax.dot_general(y, wp_ref[...], (((0,), (0,)), ((), ())),
                          preferred_element_type=F32)
    for rr in range(R_BLK):
        g = g_ref[0, :, rr * D_MODEL:(rr + 1) * D_MODEL]
        o_ref[0, :, rr * D_MODEL:(rr + 1) * D_MODEL] = (
            jax.nn.sigmoid(g) * ys5[rr * Q_S5:(rr + 1) * Q_S5, :])


def _lru_gates(xc_ref, hp, wa_ref, ba_ref, wx_ref, bx_ref, cv_ref, nq):
    c0 = pl.multiple_of(hp * LRU_PAIR, LRU_PAIR)
    xs = [xc_ref[0, :, pl.ds(pl.multiple_of(r * D_MODEL + c0, LRU_PAIR), LRU_PAIR)]
          for r in range(SUB)]
    xc = jnp.concatenate(xs, axis=0)
    xb = xc.astype(BF16)

    ra = jnp.dot(xb, wa_ref[hp], preferred_element_type=F32) + ba_ref[hp]
    rx = jnp.dot(xb, wx_ref[hp], preferred_element_type=F32) + bx_ref[hp]
    log_a = cv_ref[hp] * jax.nn.sigmoid(ra)
    a = jnp.exp(log_a)
    b = jnp.sqrt(1.0 - a * a) * (jax.nn.sigmoid(rx) * xc)
    return a, b


def _lru_scan(a, b, h_ref, col_of, p_ref, carry_row, nq, reverse):
    order = list(range(SUB - 1, -1, -1)) if reverse else list(range(SUB))
    h = None
    p = None
    for r in order:
        ar = a[r * nq:(r + 1) * nq, :]
        br = b[r * nq:(r + 1) * nq, :]
        h = br if h is None else ar * h + br
        p = ar if p is None else ar * p
        h_ref[0, :, col_of(r)] = h
        p_ref[r] = p

    row = lax.broadcasted_iota(jnp.int32, (nq, LRU_PAIR), 0)
    s = 1
    while s < nq:
        keep = (row < nq - s) if reverse else (row >= s)
        shift = (nq - s) if reverse else s
        hs = jnp.where(keep, pltpu.roll(h, shift, 0), 0.0)
        ps = jnp.where(keep, pltpu.roll(p, shift, 0), 1.0)
        h = h + p * hs
        p = p * ps
        s *= 2
    e = h + p * carry_row
    if reverse:
        cin = jnp.where(row < nq - 1, pltpu.roll(e, nq - 1, 0), carry_row)
        carry_out = e[0:1, :]
    else:
        cin = jnp.where(row >= 1, pltpu.roll(e, 1, 0), carry_row)
        carry_out = e[nq - 1:nq, :]
    for r in order:
        h_ref[0, :, col_of(r)] = h_ref[0, :, col_of(r)] + p_ref[r] * cin
    return carry_out


def _lru_bwd_kernel(u_ref, halo_ref, cw_ref, cbias_ref, wa_ref, ba_ref, wx_ref, bx_ref, cv_ref,
                    xc_ref, h_ref, p_ref, nx_ref, carry_ref, *, nt):
    i = pl.program_id(1)
    tile = nt - 1 - i
    nq = Q_LRU_B

    @pl.when(i == 0)
    def _():
        nx_ref[...] = jnp.zeros_like(nx_ref)
        carry_ref[...] = jnp.zeros_like(carry_ref)

    def cls(r):
        return u_ref[0, :, r * D_MODEL:(r + 1) * D_MODEL]

    row = lax.broadcasted_iota(jnp.int32, (nq, D_MODEL), 0)
    live = (tile > 0).astype(F32)
    hp = halo_ref[0, SUBLANES - 1:SUBLANES, :] * live
    d14 = jnp.where(row == 0, hp[:, 14 * D_MODEL:15 * D_MODEL], pltpu.roll(cls(14), 1, 0))
    d15 = jnp.where(row == 0, hp[:, 15 * D_MODEL:16 * D_MODEL], pltpu.roll(cls(15), 1, 0))
    u0 = jnp.where(row == nq - 1, nx_ref[0:1, :], pltpu.roll(cls(0), nq - 1, 0))
    nx_new = u_ref[0, 0:1, 0:D_MODEL]

    def tap(r):
        if r == -2:
            return d14
        if r == -1:
            return d15
        if r == SUB:
            return u0
        return cls(r)

    w = cw_ref[...]
    for r in range(SUB):
        xc = (cbias_ref[...] + tap(r - 2) * w[0:1] + tap(r - 1) * w[1:2]
              + tap(r) * w[2:3] + tap(r + 1) * w[3:4])
        xc_ref[0, :, r * D_MODEL:(r + 1) * D_MODEL] = xc
    nx_ref[0:1, :] = nx_new

    def head_pair(hp, c):
        c0 = pl.multiple_of(hp * LRU_PAIR, LRU_PAIR)
        a, b = _lru_gates(xc_ref, hp, wa_ref, ba_ref, wx_ref, bx_ref, cv_ref, nq)

        def col_of(r):
            return pl.ds(pl.multiple_of(r * D_MODEL + c0, LRU_PAIR), LRU_PAIR)
        cout = _lru_scan(a, b, h_ref, col_of, p_ref,
                         carry_ref[0:1, pl.ds(c0, LRU_PAIR)], nq, True)
        carry_ref[0:1, pl.ds(c0, LRU_PAIR)] = cout
        return c
    lax.fori_loop(0, LRU_HEADS // 2, head_pair, 0)


def _out_kernel(x_ref, xc_ref, hb_ref, z_ref, g_ref, m_ref, wa_ref, ba_ref, wx_ref, bx_ref, cv_ref,
                wl_ref, wo_ref, gf_ref, perm_ref, o_ref, hf_ref, p_ref, carry_ref, v_ref):
    i = pl.program_id(1)
    nq = Q_OUT

    @pl.when(i == 0)
    def _():
        carry_ref[...] = jnp.zeros_like(carry_ref)

    def head_pair(hp, c):
        c0 = pl.multiple_of(hp * LRU_PAIR, LRU_PAIR)
        a, b = _lru_gates(xc_ref, hp, wa_ref, ba_ref, wx_ref, bx_ref, cv_ref, nq)

        def col_of(r):
            return pl.ds(pl.multiple_of(r * D_MODEL + c0, LRU_PAIR), LRU_PAIR)
        cout = _lru_scan(a, b, hf_ref, col_of, p_ref,
                         carry_ref[0:1, pl.ds(c0, LRU_PAIR)], nq, False)
        carry_ref[0:1, pl.ds(c0, LRU_PAIR)] = cout
        for r in range(SUB):
            hl = hf_ref[0, :, col_of(r)] + hb_ref[0, :, col_of(r)]
            v = hl * jax.nn.silu(z_ref[0, :, col_of(r)])
            v_ref[r * nq:(r + 1) * nq, pl.ds(c0, LRU_PAIR)] = v.astype(BF16)
        return c
    lax.fori_loop(0, LRU_HEADS // 2, head_pair, 0)

    ylru = jnp.dot(v_ref[...], wl_ref[...], preferred_element_type=F32)
    ms = []
    for r in range(SUB):
        sl = slice(r * D_MODEL, (r + 1) * D_MODEL)
        ms.append((m_ref[0, :, sl] + jax.nn.sigmoid(g_ref[0, :, sl]) * ylru[r * nq:(r + 1) * nq, :])
                  .astype(BF16))
    for gi in range(nq // SUB):
        mg = jnp.concatenate([m[gi * SUB:(gi + 1) * SUB, :] for m in ms], axis=0)
        mn = _class_major(mg, perm_ref[...])
        dm = jnp.dot(mn, wo_ref[...], preferred_element_type=F32)
        xo = x_ref[0, gi * PERM:(gi + 1) * PERM, :] + dm
        o_ref[0, gi * PERM:(gi + 1) * PERM, :] = _rms(xo, gf_ref[...])


def _cx_mul(a, b):
    return a[0] * b[0] - a[1] * b[1], a[0] * b[1] + a[1] * b[0]


def _prep_s5(a_re, a_im, log_dt, b_re, b_im, c_re, c_im):
    hi = lax.Precision.HIGHEST
    g = S5_GROUPS
    are, aim = a_re.astype(F32), a_im.astype(F32)
    dt = jnp.exp(log_dt.astype(F32))[..., None]
    lre, lim = are * dt, aim * dt

    def apow(n):
        nn = n.astype(F32)[None, None, :, None]
        mag = jnp.exp(lre[:, :, None, :] * nn)
        ang = lim[:, :, None, :] * nn
        return mag * jnp.cos(ang), mag * jnp.sin(ang)

    ab = (jnp.exp(lre) * jnp.cos(lim), jnp.exp(lre) * jnp.sin(lim))
    den = are * are + aim * aim
    quo = (((ab[0] - 1.0) * are + ab[1] * aim) / den, (ab[1] * are - (ab[0] - 1.0) * aim) / den)
    bbar = _cx_mul((quo[0][..., None], quo[1][..., None]), (b_re.astype(F32), b_im.astype(F32)))
    cmat = (c_re.astype(F32), c_im.astype(F32))

    pw = apow(jnp.arange(SUB + 1))
    cp = _cx_mul((cmat[0][:, :, None], cmat[1][:, :, None]),
                 (pw[0][:, :, :SUB, None, :], pw[1][:, :, :SUB, None, :]))
    kern = (jnp.einsum('dgkip,dgpj->dgkij', cp[0], bbar[0], precision=hi)
            - jnp.einsum('dgkip,dgpj->dgkij', cp[1], bbar[1], precision=hi))
    kf, kb = kern[0], kern[1]
    k16 = jnp.arange(SUB)
    rp, rr = k16[:, None], k16[None, :]
    idx = jnp.abs(rp - rr)
    kfe, kbe = kf[:, idx], kb[:, idx]
    up = (rp > rr)[None, :, :, None, None]
    lo = (rp < rr)[None, :, :, None, None]
    m = jnp.where(up, kfe, jnp.where(lo, kbe, kfe + kbe))
    mt = m.transpose(0, 1, 3, 2, 4).reshape(g, 256, 256)

    def cat(zf, zb, sign=1.0):
        return jnp.concatenate([zf[0], zb[0], sign * zf[1], sign * zb[1]], axis=-1)

    def sel(z, d, fn):
        return fn(z[0][d]), fn(z[1][d])

    bt = [(bbar[0][d].transpose(0, 2, 1)[:, None], bbar[1][d].transpose(0, 2, 1)[:, None])
          for d in range(2)]
    wf = _cx_mul(sel(pw, 0, lambda z: z[:, ::-1][:, 1:, None, :]), bt[0])
    wb = _cx_mul(sel(pw, 1, lambda z: z[:, :SUB, None, :]), bt[1])
    wbt = cat(wf, wb).reshape(g, 256, 256).transpose(0, 2, 1)
    cf = _cx_mul(sel(cmat, 0, lambda z: z[:, None]), sel(pw, 0, lambda z: z[:, 1:, None, :]))
    cb = _cx_mul(sel(cmat, 1, lambda z: z[:, None]),
                 sel(pw, 1, lambda z: z[:, ::-1][:, :SUB, None, :]))
    wct = cat(cf, cb, -1.0).reshape(g, 256, 256)

    def tab(nf, nb, mf, mb):
        zf = sel(apow(nf), 0, lambda z: z * mf[None, :, None])
        zb = sel(apow(nb), 1, lambda z: z * mb[None, :, None])
        return cat(zf, zb)

    m8 = jnp.arange(SUBLANES)
    nblk = Q_S5 // SUBLANES
    m16 = jnp.arange(nblk)
    parts = []
    for s in (1, 2, 4):
        n = jnp.full((SUBLANES,), SUB * s)
        parts.append(tab(n, n, (m8 >= s).astype(F32), (m8 < SUBLANES - s).astype(F32)))
    for s in (1, 2, 4, 8):
        n = jnp.full((nblk,), SUB * SUBLANES * s)
        parts.append(tab(n, n, (m16 >= s).astype(F32), (m16 < nblk - s).astype(F32)))
    one8 = jnp.ones((SUBLANES,), F32)
    parts.append(tab(SUB * (m8 + 1), SUB * (SUBLANES - m8), one8, one8))
    for n in (SUB, SUB * Q_S5):
        nn = jnp.full((SUBLANES,), n)
        parts.append(tab(nn, nn, one8, one8))
    table = jnp.concatenate(parts, axis=1)
    nq = SUB * jnp.arange(Q_S5)
    wq = tab(nq, nq, jnp.zeros((Q_S5,), F32), jnp.ones((Q_S5,), F32))
    return mt.astype(BF16), wbt.astype(BF16), wct.astype(BF16), table, wq


def _pair_heads(w):
    w = w.astype(BF16).reshape(2, LRU_HEADS // 2, 2, LRU_BLOCK, LRU_BLOCK)
    z = jnp.zeros_like(w[:, :, 0])
    top = jnp.concatenate([w[:, :, 0], z], axis=-1)
    bot = jnp.concatenate([z, w[:, :, 1]], axis=-1)
    return jnp.concatenate([top, bot], axis=-2)


def _full(shape):
    return pl.BlockSpec(shape, lambda *_: (0,) * len(shape))


def _params(sem, **kw):
    return pltpu.CompilerParams(dimension_semantics=sem, vmem_limit_bytes=VMEM_LIMIT, **kw)


def _trunk(x, w):
    bsz, seq, _ = x.shape
    lq = seq // SUB
    assert seq % (SUB * Q_S5) == 0
    row_shape = jax.ShapeDtypeStruct((bsz, lq, SUB * D_MODEL), F32)

    nt = lq // Q_S5
    uz = pl.pallas_call(
        _in_t_kernel,
        grid=(bsz, nt, 2),
        in_specs=[pl.BlockSpec((1, SUB * Q_S5, D_MODEL), lambda b, t, c: (b, t, 0)),
                  _full((1, D_MODEL)), _full((PERM, PERM)),
                  pl.BlockSpec((D_MODEL, D_MODEL), lambda b, t, c: (c, 0))],
        out_specs=pl.BlockSpec((1, SUB, D_MODEL, Q_S5), lambda b, t, c: (b, 0, c, t)),
        out_shape=jax.ShapeDtypeStruct((bsz, SUB, 2 * D_MODEL, lq), F32),
        scratch_shapes=[pltpu.VMEM((SUB * Q_S5, D_MODEL), BF16)],
        compiler_params=_params(("parallel", "parallel", "arbitrary")),
        name="in_proj_t",
    )(x, w["norm_g"], w["perm"], w["w_s5_t"])

    rows = pl.BlockSpec((1, Q_IN, SUB * D_MODEL), lambda b, t: (b, t, 0))
    u_lru, z_lru, g_s5, g_lru = pl.pallas_call(
        _in_n_kernel,
        grid=(bsz, lq // Q_IN),
        in_specs=[pl.BlockSpec((1, PERM, D_MODEL), lambda b, t: (b, t, 0)),
                  _full((1, D_MODEL)), _full((PERM, PERM)), _full((D_MODEL, 4 * D_MODEL))],
        out_specs=[rows] * 4,
        out_shape=[row_shape] * 4,
        compiler_params=_params(("parallel", "parallel")),
        name="in_proj_n",
    )(x, w["norm_g"], w["perm"], w["w_rest"])

    def tile_of(ph, i):
        return jnp.where(ph == 0, nt - 1 - i, i)
    gw = pl.BlockSpec((GROUPS_PER_STEP, 256, 256), lambda g, b, ph, i: (g, 0, 0))
    y_t = pl.pallas_call(
        functools.partial(_s5_kernel, nt=nt),
        grid=(S5_GROUPS // GROUPS_PER_STEP, bsz, 2, nt),
        in_specs=[pl.BlockSpec((1, SUB, LANES, Q_S5), lambda g, b, ph, i: (b, 0, g, tile_of(ph, i))),
                  pl.BlockSpec((LANES, LANES), lambda g, b, ph, i: (g, 0)),
                  gw, gw, gw,
                  pl.BlockSpec((GROUPS_PER_STEP, T_ROWS, 2 * LANES), lambda g, b, ph, i: (g, 0, 0)),
                  pl.BlockSpec((GROUPS_PER_STEP, Q_S5, 2 * LANES), lambda g, b, ph, i: (g, 0, 0))],
        out_specs=pl.BlockSpec((1, SUB, LANES, Q_S5), lambda g, b, ph, i: (b, 0, g, ph * i)),
        out_shape=jax.ShapeDtypeStruct((bsz, SUB, D_MODEL, lq), F32),
        scratch_shapes=[pltpu.VMEM((GROUPS_PER_STEP, Q_S5, 2 * LANES), F32),
                        pltpu.VMEM((GROUPS_PER_STEP, SUBLANES, 2 * LANES), F32),
                        pltpu.VMEM((GROUPS_PER_STEP, SUBLANES, 2 * LANES), F32),
                        pltpu.VMEM((nt, GROUPS_PER_STEP, SUBLANES, 2 * LANES), F32)],
        compiler_params=_params(("arbitrary",) * 4),
        name="s5_ssm",
    )(uz, w["s5_d_b"], w["s5_mt"], w["s5_wbt"], w["s5_wct"], w["s5_tab"], w["s5_wq"])

    m_s5 = pl.pallas_call(
        _s5_post_kernel,
        grid=(bsz, nt, SUB // R_BLK),
        in_specs=[pl.BlockSpec((1, R_BLK, D_MODEL, Q_S5), lambda b, t, r: (b, r, 0, t)),
                  pl.BlockSpec((1, R_BLK, D_MODEL, Q_S5), lambda b, t, r: (b, r, 1, t)),
                  pl.BlockSpec((1, Q_S5, R_BLK * D_MODEL), lambda b, t, r: (b, t, r)),
                  _full((D_MODEL, D_MODEL)), _full((D_MODEL, LANES)), _full((D_MODEL, D_MODEL))],
        out_specs=pl.BlockSpec((1, Q_S5, R_BLK * D_MODEL), lambda b, t, r: (b, t, r)),
        out_shape=row_shape,
        compiler_params=_params(("parallel", "parallel", "parallel")),
        name="s5_post",
    )(y_t, uz, g_s5, w["glu_wt"], w["glu_b_b"], w["s5_proj"])

    ntb = lq // Q_LRU_B
    hpb = Q_LRU_B // SUBLANES
    rows_b = pl.BlockSpec((1, Q_LRU_B, SUB * D_MODEL), lambda b, i: (b, ntb - 1 - i, 0))
    gate_w = _full((LRU_HEADS // 2, LRU_PAIR, LRU_PAIR))
    gate_b = _full((LRU_HEADS // 2, 1, LRU_PAIR))
    xc, h_bwd = pl.pallas_call(
        functools.partial(_lru_bwd_kernel, nt=ntb),
        grid=(bsz, ntb),
        in_specs=[rows_b,
                  pl.BlockSpec((1, SUBLANES, SUB * D_MODEL),
                               lambda b, i: (b, jnp.maximum((ntb - 1 - i) * hpb - 1, 0), 0)),
                  _full((4, D_MODEL)), _full((1, D_MODEL)),
                  gate_w, gate_b, gate_w, gate_b, gate_b],
        out_specs=[rows_b, rows_b],
        out_shape=[row_shape, row_shape],
        scratch_shapes=[pltpu.VMEM((SUB, Q_LRU_B, LRU_PAIR), F32),
                        pltpu.VMEM((SUBLANES, D_MODEL), F32),
                        pltpu.VMEM((SUBLANES, D_MODEL), F32)],
        compiler_params=_params(("arbitrary", "arbitrary")),
        name="lru_bwd",
    )(u_lru, u_lru, w["conv_w"], w["conv_b"], w["wa"][1], w["ba"][1], w["wx"][1], w["bx"][1],
      w["cvec"][1])

    rows_o = pl.BlockSpec((1, Q_OUT, SUB * D_MODEL), lambda b, i: (b, i, 0))
    x_o = pl.BlockSpec((1, SUB * Q_OUT, D_MODEL), lambda b, i: (b, i, 0))
    out = pl.pallas_call(
        _out_kernel,
        grid=(bsz, lq // Q_OUT),
        in_specs=[x_o] + [rows_o] * 5 + [gate_w, gate_b, gate_w, gate_b, gate_b,
                                         _full((D_MODEL, D_MODEL)), _full((D_MODEL, D_MODEL)),
                                         _full((1, D_MODEL)), _full((PERM, PERM))],
        out_specs=x_o,
        out_shape=jax.ShapeDtypeStruct((bsz, seq, D_MODEL), F32),
        scratch_shapes=[pltpu.VMEM((1, Q_OUT, SUB * D_MODEL), F32),
                        pltpu.VMEM((SUB, Q_OUT, LRU_PAIR), F32),
                        pltpu.VMEM((SUBLANES, D_MODEL), F32),
                        pltpu.VMEM((SUB * Q_OUT, D_MODEL), BF16)],
        compiler_params=_params(("arbitrary", "arbitrary")),
        name="lru_fwd_out",
    )(x, xc, h_bwd, z_lru, g_lru, m_s5, w["wa"][0], w["ba"][0], w["wx"][0], w["bx"][0],
      w["cvec"][0], w["lru_proj"], w["w_out"], w["norm_f_g"], w["perm"])
    return out


def kernel(x_prompt, x_sample, norm_g, w_in, s5_a_re, s5_a_im, s5_log_dt, s5_b_re, s5_b_im, s5_c_re, s5_c_im, s5_d, s5_glu_w, s5_glu_b, s5_proj, lru_conv_w, lru_conv_b, lru_lambda, lru_wa, lru_ba, lru_wx, lru_bx, lru_proj, w_out, norm_f_g):
    assert norm_g.shape[0] == 1, "single-layer trunk"
    wi = w_in[0]
    mt, wbt, wct, tab, wq = _prep_s5(s5_a_re[0], s5_a_im[0], s5_log_dt[0], s5_b_re[0], s5_b_im[0],
                                     s5_c_re[0], s5_c_im[0])
    tok = jnp.arange(PERM)
    w = {
        "perm": (tok[None, :] == SUB * (tok % SUB)[:, None] + (tok // SUB)[:, None]).astype(BF16),
        "norm_g": norm_g[0].reshape(1, D_MODEL).astype(F32),
        "w_s5_t": wi[:, :2 * D_MODEL].T.astype(BF16),
        "w_rest": wi[:, 2 * D_MODEL:].astype(BF16),
        "s5_mt": mt, "s5_wbt": wbt, "s5_wct": wct, "s5_tab": tab, "s5_wq": wq,
        "s5_d_b": jnp.broadcast_to(s5_d[0].astype(F32)[:, None], (D_MODEL, LANES)),
        "glu_wt": s5_glu_w[0].T.astype(BF16),
        "glu_b_b": jnp.broadcast_to(s5_glu_b[0].astype(F32)[:, None], (D_MODEL, LANES)),
        "s5_proj": s5_proj[0].astype(BF16),
        "conv_w": lru_conv_w[0].astype(F32),
        "conv_b": lru_conv_b[0].reshape(1, D_MODEL).astype(F32),
        "wa": _pair_heads(lru_wa[0]),
        "wx": _pair_heads(lru_wx[0]),
        "ba": lru_ba[0].reshape(2, LRU_HEADS // 2, 1, LRU_PAIR).astype(F32),
        "bx": lru_bx[0].reshape(2, LRU_HEADS // 2, 1, LRU_PAIR).astype(F32),
        "cvec": (-RG_C * jax.nn.softplus(-lru_lambda[0].astype(F32))).reshape(2, LRU_HEADS // 2, 1, LRU_PAIR),
        "lru_proj": lru_proj[0].astype(BF16),
        "w_out": w_out[0].astype(BF16),
        "norm_f_g": norm_f_g.reshape(1, D_MODEL).astype(F32),
    }
    return (_trunk(x_prompt, w), _trunk(x_sample, w))
```

```python
import functools

import jax
import jax.numpy as jnp
from jax import lax
from jax.experimental import pallas as pl
from jax.experimental.pallas import tpu as pltpu

F32 = jnp.float32
BF16 = jnp.bfloat16

D_MODEL = 1024
SUB = 16
PERM = SUB * SUB
S5_GROUPS = 64
S5_GROUP_SIZE = 16
S5_STATE = 64
GROUPS_PER_STEP = 8
LRU_HEADS = 8
LRU_BLOCK = 128
LRU_PAIR = 2 * LRU_BLOCK
RG_C = 8.0
EPS = 1e-6
LANES = 128
SUBLANES = 8
Q_S5 = 128
R_BLK = 4
Q_IN = 64
Q_LRU_B = 64
Q_OUT = 32
VMEM_LIMIT = 56 * 1024 * 1024

T_STEP = 0
T_BLK = 24
T_FIX = 88
T_A16 = 96
T_AQ = 104
T_ROWS = 112


def _rms(x, g):
    return x * lax.rsqrt(jnp.mean(x * x, axis=-1, keepdims=True) + EPS) * g


def _cmul_cat(c, x):
    cre, cim = c[:, :LANES], c[:, LANES:]
    xre, xim = x[:, :LANES], x[:, LANES:]
    return jnp.concatenate([cre * xre - cim * xim, cre * xim + cim * xre], axis=1)


def _class_major(h, perm):
    return jnp.dot(perm, h, preferred_element_type=F32).astype(BF16)


def _in_t_kernel(x_ref, g_ref, perm_ref, w_ref, o_ref, h_ref):
    @pl.when(pl.program_id(2) == 0)
    def _():
        _class_major_rows(x_ref, g_ref, perm_ref, h_ref, Q_S5)

    rows = R_BLK * Q_S5
    for rb in range(SUB // R_BLK):
        res = lax.dot_general(w_ref[...], h_ref[rb * rows:(rb + 1) * rows, :],
                              (((1,), (1,)), ((), ())), preferred_element_type=F32)
        for rr in range(R_BLK):
            o_ref[0, rb * R_BLK + rr] = res[:, rr * Q_S5:(rr + 1) * Q_S5]


def _in_n_kernel(x_ref, g_ref, perm_ref, w_ref, o_ref):
    for gi in range(Q_IN // SUB):
        xg = x_ref[0, gi * PERM:(gi + 1) * PERM, :]
        h = _class_major(_rms(xg, g_ref[...]).astype(BF16), perm_ref[...])
        res = jnp.dot(h, w_ref[...], preferred_element_type=F32)
        for r in range(SUB):
            o_ref[0, gi * SUB:(gi + 1) * SUB, r * D_MODEL:(r + 1) * D_MODEL] = res[r * SUB:(r + 1) * SUB, :]


def _s5_kernel(u_ref, d_ref, mt_ref, wbt_ref, wct_ref, tab_ref, wq_ref, o_ref,
               s_ref, cf_ref, cb_ref, hb_ref, *, nt):
    ph = pl.program_id(2)
    i = pl.program_id(3)
    tile = jnp.where(ph == 0, nt - 1 - i, i)
    q = Q_S5
    nblk = q // SUBLANES

    @pl.when(i == 0)
    def _():
        cf_ref[...] = jnp.zeros_like(cf_ref)
        cb_ref[...] = jnp.zeros_like(cb_ref)

    def lane_fwd(rows):
        lane = lax.broadcasted_iota(jnp.int32, (rows, 2 * LANES), 1)
        return (lane % LANES) < S5_STATE

    def load_x(gl):
        c0 = gl * S5_GROUP_SIZE
        xt = u_ref[0, :, c0:c0 + S5_GROUP_SIZE, :]
        return c0, xt, xt.reshape(SUB * S5_GROUP_SIZE, q).astype(BF16)

    def states(gl, xb):
        st = jnp.dot(wbt_ref[gl], xb, preferred_element_type=F32)
        return st.T

    @pl.when(ph == 0)
    def _():
        cbs = [cb_ref[gl] for gl in range(GROUPS_PER_STEP)]
        for gl in range(GROUPS_PER_STEP):
            _, _, xb = load_x(gl)
            s = states(gl, xb)
            hb_ref[tile, gl] = cbs[gl]
            ws = _cmul_cat(wq_ref[gl], s)
            red = jnp.sum(ws, axis=0, keepdims=True)
            cb_ref[gl] = (jnp.broadcast_to(red, (SUBLANES, 2 * LANES))
                          + _cmul_cat(tab_ref[gl, T_AQ:T_AQ + SUBLANES, :], cbs[gl]))

    @pl.when(ph == 1)
    def _():
        fwd8 = lane_fwd(SUBLANES)
        fwd16 = lane_fwd(nblk)
        row8 = lax.broadcasted_iota(jnp.int32, (SUBLANES, 2 * LANES), 0)
        row16 = lax.broadcasted_iota(jnp.int32, (nblk, 2 * LANES), 0)
        rowq = lax.broadcasted_iota(jnp.int32, (q, 2 * LANES), 0)
        fwdq = lane_fwd(q)

        cfs = [cf_ref[gl] for gl in range(GROUPS_PER_STEP)]
        cbs = [hb_ref[tile, gl] for gl in range(GROUPS_PER_STEP)]

        def stage_states(gl):
            c0, xt, xb = load_x(gl)
            sc = s_ref.at[gl]
            sc[...] = states(gl, xb)
            y = jnp.dot(mt_ref[gl], xb, preferred_element_type=F32)
            dv = d_ref[c0:c0 + S5_GROUP_SIZE, :]
            o_ref[0, :, c0:c0 + S5_GROUP_SIZE, :] = (
                y.reshape(SUB, S5_GROUP_SIZE, q) + xt * dv[None])

        def stage_scan(gl):
            sc = s_ref.at[gl]
            cf_in = cfs[gl]
            cb_in = cbs[gl]
            a16 = tab_ref[gl, T_A16:T_A16 + SUBLANES, :]
            sc[0:SUBLANES, :] += jnp.where(fwd8 & (row8 == 0), _cmul_cat(a16, cf_in), 0.0)
            sc[q - SUBLANES:q, :] += jnp.where((~fwd8) & (row8 == SUBLANES - 1),
                                               _cmul_cat(a16, cb_in), 0.0)

            steps = [tab_ref[gl, T_STEP + 8 * si:T_STEP + 8 * si + 8, :] for si in range(3)]
            ends = []
            for k in range(nblk):
                b = sc[k * SUBLANES:(k + 1) * SUBLANES, :]
                for si, s in enumerate((1, 2, 4)):
                    sh = jnp.where(fwd8, pltpu.roll(b, s, 0), pltpu.roll(b, SUBLANES - s, 0))
                    b = b + _cmul_cat(steps[si], sh)
                sc[k * SUBLANES:(k + 1) * SUBLANES, :] = b
                ends.append(jnp.where(fwd8[0:1], b[SUBLANES - 1:SUBLANES, :], b[0:1, :]))

            e = jnp.concatenate(ends, axis=0)
            for si, s in enumerate((1, 2, 4, 8)):
                sh = jnp.where(fwd16, pltpu.roll(e, s, 0), pltpu.roll(e, nblk - s, 0))
                e = e + _cmul_cat(tab_ref[gl, T_BLK + 16 * si:T_BLK + 16 * si + 16, :], sh)
            eprev = jnp.where(fwd16,
                              jnp.where(row16 >= 1, pltpu.roll(e, 1, 0), 0.0),
                              jnp.where(row16 <= nblk - 2, pltpu.roll(e, nblk - 1, 0), 0.0))
            fix = tab_ref[gl, T_FIX:T_FIX + SUBLANES, :]
            for k in range(nblk):
                ev = jnp.broadcast_to(eprev[k:k + 1, :], (SUBLANES, 2 * LANES))
                sc[k * SUBLANES:(k + 1) * SUBLANES, :] += _cmul_cat(fix, ev)
            cf_ref[gl] = jnp.broadcast_to(e[nblk - 1:nblk, :], (SUBLANES, 2 * LANES))

        def stage_out(gl):
            c0 = gl * S5_GROUP_SIZE
            h = s_ref[gl]
            dn = jnp.where(rowq == 0, jnp.broadcast_to(cfs[gl][0:1, :], (q, 2 * LANES)),
                           pltpu.roll(h, 1, 0))
            up = jnp.where(rowq == q - 1, jnp.broadcast_to(cbs[gl][0:1, :], (q, 2 * LANES)),
                           pltpu.roll(h, q - 1, 0))
            hprev = jnp.where(fwdq, dn, up).astype(BF16)
            y = lax.dot_general(wct_ref[gl], hprev, (((1,), (1,)), ((), ())),
                                preferred_element_type=F32)
            o_ref[0, :, c0:c0 + S5_GROUP_SIZE, :] += y.reshape(SUB, S5_GROUP_SIZE, q)

        for step in range(GROUPS_PER_STEP + 2):
            if step < GROUPS_PER_STEP:
                stage_states(step)
            if 0 <= step - 1 < GROUPS_PER_STEP:
                stage_scan(step - 1)
            if 0 <= step - 2 < GROUPS_PER_STEP:
                stage_out(step - 2)


def _s5_post_kernel(y_ref, z_ref, g_ref, wg_ref, bg_ref, wp_ref, o_ref):
    ys = [jax.nn.gelu(y_ref[0, rr]) for rr in range(R_BLK)]
    y = jnp.concatenate(ys, axis=1)
    glu = jnp.dot(wg_ref[...], y.astype(BF16), preferred_element_type=F32)
    bias = jnp.concatenate([bg_ref[...]] * R_BLK, axis=1)
    y = y * jax.nn.sigmoid(glu + bias)
    z = jnp.concatenate([z_ref[0, rr] for rr in range(R_BLK)], axis=1)
    y = (y * jax.nn.silu(z)).astype(BF16)
    ys5 = lax.dot_general(y, wp_ref[...], (((0,), (0,)), ((), ())),
                          preferred_element_type=F32)
    for rr in range(R_BLK):
        g = g_ref[0, :, rr * D_MODEL:(rr + 1) * D_MODEL]
        o_ref[0, :, rr * D_MODEL:(rr + 1) * D_MODEL] = (
            jax.nn.sigmoid(g) * ys5[rr * Q_S5:(rr + 1) * Q_S5, :])


def _pair_cols(hp, r):
    return slice(r * D_MODEL + hp * LRU_PAIR, r * D_MODEL + (hp + 1) * LRU_PAIR)


def _lru_gates(xc_ref, hp, wa_ref, ba_ref, wx_ref, bx_ref, cv_ref, nq):
    xc = jnp.concatenate([xc_ref[0, :, _pair_cols(hp, r)] for r in range(SUB)], axis=0)
    xb = xc.astype(BF16)

    ra = jnp.dot(xb, wa_ref[hp], preferred_element_type=F32) + ba_ref[hp]
    rx = jnp.dot(xb, wx_ref[hp], preferred_element_type=F32) + bx_ref[hp]
    log_a = cv_ref[hp] * jax.nn.sigmoid(ra)
    a = jnp.exp(log_a)
    b = jnp.sqrt(1.0 - a * a) * (jax.nn.sigmoid(rx) * xc)
    return a, b


def _lru_scan(a, b, h_ref, col_of, p_ref, carry_row, nq, reverse):
    order = list(range(SUB - 1, -1, -1)) if reverse else list(range(SUB))
    h = None
    p = None
    for r in order:
        ar = a[r * nq:(r + 1) * nq, :]
        br = b[r * nq:(r + 1) * nq, :]
        h = br if h is None else ar * h + br
        p = ar if p is None else ar * p
        h_ref[0, :, col_of(r)] = h
        p_ref[r] = p

    row = lax.broadcasted_iota(jnp.int32, (nq, LRU_PAIR), 0)
    s = 1
    while s < nq:
        keep = (row < nq - s) if reverse else (row >= s)
        shift = (nq - s) if reverse else s
        hs = jnp.where(keep, pltpu.roll(h, shift, 0), 0.0)
        ps = jnp.where(keep, pltpu.roll(p, shift, 0), 1.0)
        h = h + p * hs
        p = p * ps
        s *= 2
    e = h + p * carry_row
    if reverse:
        cin = jnp.where(row < nq - 1, pltpu.roll(e, nq - 1, 0), carry_row)
        carry_out = e[0:1, :]
    else:
        cin = jnp.where(row >= 1, pltpu.roll(e, 1, 0), carry_row)
        carry_out = e[nq - 1:nq, :]
    for r in order:
        h_ref[0, :, col_of(r)] = h_ref[0, :, col_of(r)] + p_ref[r] * cin
    return carry_out


def _class_major_rows(x_ref, g_ref, perm_ref, h_ref, nq):
    for gi in range(nq // SUB):
        xg = x_ref[0, gi * PERM:(gi + 1) * PERM, :]
        hp = _class_major(_rms(xg, g_ref[...]).astype(BF16), perm_ref[...])
        for r in range(SUB):
            h_ref[r * nq + gi * SUB:r * nq + (gi + 1) * SUB, :] = hp[r * SUB:(r + 1) * SUB, :]


def _lru_bwd_kernel(x_ref, xh_ref, g_ref, perm_ref, wu_ref, cw_ref, cbias_ref,
                    wa_ref, ba_ref, wx_ref, bx_ref, cv_ref,
                    xc_ref, h_ref, u_ref, uh_ref, hn_ref, p_ref, nx_ref, carry_ref, *, nt):
    i = pl.program_id(1)
    tile = nt - 1 - i
    nq = Q_LRU_B

    @pl.when(i == 0)
    def _():
        nx_ref[...] = jnp.zeros_like(nx_ref)
        carry_ref[...] = jnp.zeros_like(carry_ref)

    _class_major_rows(x_ref, g_ref, perm_ref, hn_ref, nq)
    hh = _rms(xh_ref[0], g_ref[...]).astype(BF16)
    live = (tile > 0).astype(F32)
    row = lax.broadcasted_iota(jnp.int32, (nq, LRU_PAIR), 0)
    npairs = LRU_HEADS // 2

    def project(hp):
        cols = slice(hp * LRU_PAIR, (hp + 1) * LRU_PAIR)
        for rb in range(SUB // R_BLK):
            res = jnp.dot(hn_ref[rb * R_BLK * nq:(rb + 1) * R_BLK * nq, :], wu_ref[:, cols],
                          preferred_element_type=F32)
            for rr in range(R_BLK):
                u_ref[0, :, _pair_cols(hp, rb * R_BLK + rr)] = res[rr * nq:(rr + 1) * nq, :]
        uh_ref[hp] = jnp.dot(hh, wu_ref[:, cols], preferred_element_type=F32) * live

    def work(hp):
        cols = slice(hp * LRU_PAIR, (hp + 1) * LRU_PAIR)

        def cls(r):
            return u_ref[0, :, _pair_cols(hp, r)]
        uh = uh_ref[hp]
        taps = {-2: jnp.where(row == 0, uh[14:15, :], pltpu.roll(cls(14), 1, 0)),
                -1: jnp.where(row == 0, uh[15:16, :], pltpu.roll(cls(15), 1, 0)),
                SUB: jnp.where(row == nq - 1, nx_ref[0:1, cols], pltpu.roll(cls(0), nq - 1, 0))}
        nx_ref[0:1, cols] = u_ref[0, 0:1, _pair_cols(hp, 0)]

        def tap(r):
            return taps[r] if r in taps else cls(r)
        w = cw_ref[:, cols]
        for r in range(SUB):
            xc_ref[0, :, _pair_cols(hp, r)] = (
                cbias_ref[:, cols] + tap(r - 2) * w[0:1] + tap(r - 1) * w[1:2]
                + tap(r) * w[2:3] + tap(r + 1) * w[3:4])
        a, b = _lru_gates(xc_ref, hp, wa_ref, ba_ref, wx_ref, bx_ref, cv_ref, nq)
        carry_ref[0:1, cols] = _lru_scan(a, b, h_ref, functools.partial(_pair_cols, hp), p_ref.at[hp % 2],
                                         carry_ref[0:1, cols], nq, True)

    project(0)
    for hp in range(npairs):
        if hp + 1 < npairs:
            project(hp + 1)
        work(hp)


def _out_kernel(x_ref, xc_ref, hb_ref, m_ref, ng_ref, wzg_ref, wa_ref, ba_ref, wx_ref, bx_ref, cv_ref,
                wl_ref, wo_ref, gf_ref, perm_ref, o_ref, hf_ref, p_ref, carry_ref, v_ref, hn_ref, zl_ref,
                gl_ref):
    i = pl.program_id(1)
    nq = Q_OUT

    @pl.when(i == 0)
    def _():
        carry_ref[...] = jnp.zeros_like(carry_ref)

    _class_major_rows(x_ref, ng_ref, perm_ref, hn_ref, nq)

    npairs = LRU_HEADS // 2

    def project(hp):
        cols = slice(hp * LRU_PAIR, (hp + 1) * LRU_PAIR)
        gcols = slice(D_MODEL + hp * LRU_PAIR, D_MODEL + (hp + 1) * LRU_PAIR)
        zl_ref[:, cols] = jnp.dot(hn_ref[...], wzg_ref[:, cols], preferred_element_type=F32)
        gl_ref[:, cols] = jnp.dot(hn_ref[...], wzg_ref[:, gcols], preferred_element_type=F32)

    def work(hp):
        cols = slice(hp * LRU_PAIR, (hp + 1) * LRU_PAIR)
        a, b = _lru_gates(xc_ref, hp, wa_ref, ba_ref, wx_ref, bx_ref, cv_ref, nq)
        carry_ref[0:1, cols] = _lru_scan(a, b, hf_ref, functools.partial(_pair_cols, hp), p_ref.at[hp % 2],
                                         carry_ref[0:1, cols], nq, False)
        for r in range(SUB):
            hl = hf_ref[0, :, _pair_cols(hp, r)] + hb_ref[0, :, _pair_cols(hp, r)]
            v = hl * jax.nn.silu(zl_ref[r * nq:(r + 1) * nq, cols])
            v_ref[r * nq:(r + 1) * nq, cols] = v.astype(BF16)

    project(0)
    for hp in range(npairs):
        if hp + 1 < npairs:
            project(hp + 1)
        work(hp)

    ylru = jnp.dot(v_ref[...], wl_ref[...], preferred_element_type=F32)
    ms = []
    for r in range(SUB):
        sl = slice(r * D_MODEL, (r + 1) * D_MODEL)
        rows = slice(r * nq, (r + 1) * nq)
        ms.append((m_ref[0, :, sl] + jax.nn.sigmoid(gl_ref[rows, :]) * ylru[rows, :]).astype(BF16))
    for gi in range(nq // SUB):
        mg = jnp.concatenate([m[gi * SUB:(gi + 1) * SUB, :] for m in ms], axis=0)
        mn = _class_major(mg, perm_ref[...])
        dm = jnp.dot(mn, wo_ref[...], preferred_element_type=F32)
        xo = x_ref[0, gi * PERM:(gi + 1) * PERM, :] + dm
        o_ref[0, gi * PERM:(gi + 1) * PERM, :] = _rms(xo, gf_ref[...])


def _cx_mul(a, b):
    return a[0] * b[0] - a[1] * b[1], a[0] * b[1] + a[1] * b[0]


def _prep_s5(a_re, a_im, log_dt, b_re, b_im, c_re, c_im):
    hi = lax.Precision.HIGHEST
    g = S5_GROUPS
    are, aim = a_re.astype(F32), a_im.astype(F32)
    dt = jnp.exp(log_dt.astype(F32))[..., None]
    lre, lim = are * dt, aim * dt

    def apow(n):
        nn = n.astype(F32)[None, None, :, None]
        mag = jnp.exp(lre[:, :, None, :] * nn)
        ang = lim[:, :, None, :] * nn
        return mag * jnp.cos(ang), mag * jnp.sin(ang)

    ab = (jnp.exp(lre) * jnp.cos(lim), jnp.exp(lre) * jnp.sin(lim))
    den = are * are + aim * aim
    quo = (((ab[0] - 1.0) * are + ab[1] * aim) / den, (ab[1] * are - (ab[0] - 1.0) * aim) / den)
    bbar = _cx_mul((quo[0][..., None], quo[1][..., None]), (b_re.astype(F32), b_im.astype(F32)))
    cmat = (c_re.astype(F32), c_im.astype(F32))

    pw = apow(jnp.arange(SUB + 1))
    cp = _cx_mul((cmat[0][:, :, None], cmat[1][:, :, None]),
                 (pw[0][:, :, :SUB, None, :], pw[1][:, :, :SUB, None, :]))
    kern = (jnp.einsum('dgkip,dgpj->dgkij', cp[0], bbar[0], precision=hi)
            - jnp.einsum('dgkip,dgpj->dgkij', cp[1], bbar[1], precision=hi))
    kf, kb = kern[0], kern[1]
    by_lag = jnp.concatenate([kf[:, :0:-1], kf[:, :1] + kb[:, :1], kb[:, 1:]], axis=1)
    m = jnp.stack([by_lag[:, SUB - 1 - rp:2 * SUB - 1 - rp] for rp in range(SUB)], axis=1)
    mt = m.transpose(0, 1, 3, 2, 4).reshape(g, 256, 256)

    def cat(zf, zb, sign=1.0):
        return jnp.concatenate([zf[0], zb[0], sign * zf[1], sign * zb[1]], axis=-1)

    def sel(z, d, fn):
        return fn(z[0][d]), fn(z[1][d])

    bt = [(bbar[0][d].transpose(0, 2, 1)[:, None], bbar[1][d].transpose(0, 2, 1)[:, None])
          for d in range(2)]
    wf = _cx_mul(sel(pw, 0, lambda z: z[:, ::-1][:, 1:, None, :]), bt[0])
    wb = _cx_mul(sel(pw, 1, lambda z: z[:, :SUB, None, :]), bt[1])
    wbt = cat(wf, wb).reshape(g, 256, 256).transpose(0, 2, 1)
    cf = _cx_mul(sel(cmat, 0, lambda z: z[:, None]), sel(pw, 0, lambda z: z[:, 1:, None, :]))
    cb = _cx_mul(sel(cmat, 1, lambda z: z[:, None]),
                 sel(pw, 1, lambda z: z[:, ::-1][:, :SUB, None, :]))
    wct = cat(cf, cb, -1.0).reshape(g, 256, 256)

    def tab(nf, nb, mf, mb):
        zf = sel(apow(nf), 0, lambda z: z * mf[None, :, None])
        zb = sel(apow(nb), 1, lambda z: z * mb[None, :, None])
        return cat(zf, zb)

    m8 = jnp.arange(SUBLANES)
    nblk = Q_S5 // SUBLANES
    m16 = jnp.arange(nblk)
    parts = []
    for s in (1, 2, 4):
        n = jnp.full((SUBLANES,), SUB * s)
        parts.append(tab(n, n, (m8 >= s).astype(F32), (m8 < SUBLANES - s).astype(F32)))
    for s in (1, 2, 4, 8):
        n = jnp.full((nblk,), SUB * SUBLANES * s)
        parts.append(tab(n, n, (m16 >= s).astype(F32), (m16 < nblk - s).astype(F32)))
    one8 = jnp.ones((SUBLANES,), F32)
    parts.append(tab(SUB * (m8 + 1), SUB * (SUBLANES - m8), one8, one8))
    for n in (SUB, SUB * Q_S5):
        nn = jnp.full((SUBLANES,), n)
        parts.append(tab(nn, nn, one8, one8))
    table = jnp.concatenate(parts, axis=1)
    nq = SUB * jnp.arange(Q_S5)
    wq = tab(nq, nq, jnp.zeros((Q_S5,), F32), jnp.ones((Q_S5,), F32))
    return mt.astype(BF16), wbt.astype(BF16), wct.astype(BF16), table, wq


def _pair_heads(w):
    w = w.astype(BF16).reshape(2, LRU_HEADS // 2, 2, LRU_BLOCK, LRU_BLOCK)
    z = jnp.zeros_like(w[:, :, 0])
    top = jnp.concatenate([w[:, :, 0], z], axis=-1)
    bot = jnp.concatenate([z, w[:, :, 1]], axis=-1)
    return jnp.concatenate([top, bot], axis=-2)


def _full(shape):
    return pl.BlockSpec(shape, lambda *_: (0,) * len(shape))


def _params(sem, **kw):
    return pltpu.CompilerParams(dimension_semantics=sem, vmem_limit_bytes=VMEM_LIMIT, **kw)


def _trunk(x, w):
    bsz, seq, _ = x.shape
    lq = seq // SUB
    assert seq % (SUB * Q_S5) == 0
    row_shape = jax.ShapeDtypeStruct((bsz, lq, SUB * D_MODEL), F32)

    nt = lq // Q_S5
    uz = pl.pallas_call(
        _in_t_kernel,
        grid=(bsz, nt, 2),
        in_specs=[pl.BlockSpec((1, SUB * Q_S5, D_MODEL), lambda b, t, c: (b, t, 0)),
                  _full((1, D_MODEL)), _full((PERM, PERM)),
                  pl.BlockSpec((D_MODEL, D_MODEL), lambda b, t, c: (c, 0))],
        out_specs=pl.BlockSpec((1, SUB, D_MODEL, Q_S5), lambda b, t, c: (b, 0, c, t)),
        out_shape=jax.ShapeDtypeStruct((bsz, SUB, 2 * D_MODEL, lq), F32),
        scratch_shapes=[pltpu.VMEM((SUB * Q_S5, D_MODEL), BF16)],
        compiler_params=_params(("parallel", "parallel", "arbitrary")),
        name="in_proj_t",
    )(x, w["norm_g"], w["perm"], w["w_s5_t"])

    rows = pl.BlockSpec((1, Q_IN, SUB * D_MODEL), lambda b, t: (b, t, 0))
    g_s5 = pl.pallas_call(
        _in_n_kernel,
        grid=(bsz, lq // Q_IN),
        in_specs=[pl.BlockSpec((1, SUB * Q_IN, D_MODEL), lambda b, t: (b, t, 0)),
                  _full((1, D_MODEL)), _full((PERM, PERM)), _full((D_MODEL, D_MODEL))],
        out_specs=rows,
        out_shape=row_shape,
        compiler_params=_params(("parallel", "parallel")),
        name="in_proj_n",
    )(x, w["norm_g"], w["perm"], w["w_gs5"])

    def tile_of(ph, i):
        return jnp.where(ph == 0, nt - 1 - i, i)
    gw = pl.BlockSpec((GROUPS_PER_STEP, 256, 256), lambda g, b, ph, i: (g, 0, 0))
    y_t = pl.pallas_call(
        functools.partial(_s5_kernel, nt=nt),
        grid=(S5_GROUPS // GROUPS_PER_STEP, bsz, 2, nt),
        in_specs=[pl.BlockSpec((1, SUB, LANES, Q_S5), lambda g, b, ph, i: (b, 0, g, tile_of(ph, i))),
                  pl.BlockSpec((LANES, LANES), lambda g, b, ph, i: (g, 0)),
                  gw, gw, gw,
                  pl.BlockSpec((GROUPS_PER_STEP, T_ROWS, 2 * LANES), lambda g, b, ph, i: (g, 0, 0)),
                  pl.BlockSpec((GROUPS_PER_STEP, Q_S5, 2 * LANES), lambda g, b, ph, i: (g, 0, 0))],
        out_specs=pl.BlockSpec((1, SUB, LANES, Q_S5), lambda g, b, ph, i: (b, 0, g, ph * i)),
        out_shape=jax.ShapeDtypeStruct((bsz, SUB, D_MODEL, lq), F32),
        scratch_shapes=[pltpu.VMEM((GROUPS_PER_STEP, Q_S5, 2 * LANES), F32),
                        pltpu.VMEM((GROUPS_PER_STEP, SUBLANES, 2 * LANES), F32),
                        pltpu.VMEM((GROUPS_PER_STEP, SUBLANES, 2 * LANES), F32),
                        pltpu.VMEM((nt, GROUPS_PER_STEP, SUBLANES, 2 * LANES), F32)],
        compiler_params=_params(("arbitrary",) * 4),
        name="s5_ssm",
    )(uz, w["s5_d_b"], w["s5_mt"], w["s5_wbt"], w["s5_wct"], w["s5_tab"], w["s5_wq"])

    m_s5 = pl.pallas_call(
        _s5_post_kernel,
        grid=(bsz, nt, SUB // R_BLK),
        in_specs=[pl.BlockSpec((1, R_BLK, D_MODEL, Q_S5), lambda b, t, r: (b, r, 0, t)),
                  pl.BlockSpec((1, R_BLK, D_MODEL, Q_S5), lambda b, t, r: (b, r, 1, t)),
                  pl.BlockSpec((1, Q_S5, R_BLK * D_MODEL), lambda b, t, r: (b, t, r)),
                  _full((D_MODEL, D_MODEL)), _full((D_MODEL, LANES)), _full((D_MODEL, D_MODEL))],
        out_specs=pl.BlockSpec((1, Q_S5, R_BLK * D_MODEL), lambda b, t, r: (b, t, r)),
        out_shape=row_shape,
        compiler_params=_params(("parallel", "parallel", "parallel")),
        name="s5_post",
    )(y_t, uz, g_s5, w["glu_wt"], w["glu_b_b"], w["s5_proj"])

    ntb = lq // Q_LRU_B
    rows_b = pl.BlockSpec((1, Q_LRU_B, SUB * D_MODEL), lambda b, i: (b, ntb - 1 - i, 0))
    gate_w = _full((LRU_HEADS // 2, LRU_PAIR, LRU_PAIR))
    gate_b = _full((LRU_HEADS // 2, 1, LRU_PAIR))
    xc, h_bwd = pl.pallas_call(
        functools.partial(_lru_bwd_kernel, nt=ntb),
        grid=(bsz, ntb),
        in_specs=[pl.BlockSpec((1, SUB * Q_LRU_B, D_MODEL), lambda b, i: (b, ntb - 1 - i, 0)),
                  pl.BlockSpec((1, SUB, D_MODEL),
                               lambda b, i: (b, jnp.maximum((ntb - 1 - i) * Q_LRU_B - 1, 0), 0)),
                  _full((1, D_MODEL)), _full((PERM, PERM)), _full((D_MODEL, D_MODEL)),
                  _full((4, D_MODEL)), _full((1, D_MODEL)),
                  gate_w, gate_b, gate_w, gate_b, gate_b],
        out_specs=[rows_b, rows_b],
        out_shape=[row_shape, row_shape],
        scratch_shapes=[pltpu.VMEM((1, Q_LRU_B, SUB * D_MODEL), F32),
                        pltpu.VMEM((LRU_HEADS // 2, SUB, LRU_PAIR), F32),
                        pltpu.VMEM((SUB * Q_LRU_B, D_MODEL), BF16),
                        pltpu.VMEM((2, SUB, Q_LRU_B, LRU_PAIR), F32),
                        pltpu.VMEM((SUBLANES, D_MODEL), F32),
                        pltpu.VMEM((SUBLANES, D_MODEL), F32)],
        compiler_params=_params(("arbitrary", "arbitrary")),
        name="lru_bwd",
    )(x, x, w["norm_g"], w["perm"], w["w_ulru"], w["conv_w"], w["conv_b"],
      w["wa"][1], w["ba"][1], w["wx"][1], w["bx"][1], w["cvec"][1])

    rows_o = pl.BlockSpec((1, Q_OUT, SUB * D_MODEL), lambda b, i: (b, i, 0))
    x_o = pl.BlockSpec((1, SUB * Q_OUT, D_MODEL), lambda b, i: (b, i, 0))
    out = pl.pallas_call(
        _out_kernel,
        grid=(bsz, lq // Q_OUT),
        in_specs=[x_o] + [rows_o] * 3 + [_full((1, D_MODEL)), _full((D_MODEL, 2 * D_MODEL)),
                                         gate_w, gate_b, gate_w, gate_b, gate_b,
                                         _full((D_MODEL, D_MODEL)), _full((D_MODEL, D_MODEL)),
                                         _full((1, D_MODEL)), _full((PERM, PERM))],
        out_specs=x_o,
        out_shape=jax.ShapeDtypeStruct((bsz, seq, D_MODEL), F32),
        scratch_shapes=[pltpu.VMEM((1, Q_OUT, SUB * D_MODEL), F32),
                        pltpu.VMEM((2, SUB, Q_OUT, LRU_PAIR), F32),
                        pltpu.VMEM((SUBLANES, D_MODEL), F32),
                        pltpu.VMEM((SUB * Q_OUT, D_MODEL), BF16),
                        pltpu.VMEM((SUB * Q_OUT, D_MODEL), BF16),
                        pltpu.VMEM((SUB * Q_OUT, D_MODEL), F32),
                        pltpu.VMEM((SUB * Q_OUT, D_MODEL), F32)],
        compiler_params=_params(("arbitrary", "arbitrary")),
        name="lru_fwd_out",
    )(x, xc, h_bwd, m_s5, w["norm_g"], w["w_zg"], w["wa"][0], w["ba"][0], w["wx"][0], w["bx"][0],
      w["cvec"][0], w["lru_proj"], w["w_out"], w["norm_f_g"], w["perm"])
    return out


def kernel(x_prompt, x_sample, norm_g, w_in, s5_a_re, s5_a_im, s5_log_dt, s5_b_re, s5_b_im, s5_c_re, s5_c_im, s5_d, s5_glu_w, s5_glu_b, s5_proj, lru_conv_w, lru_conv_b, lru_lambda, lru_wa, lru_ba, lru_wx, lru_bx, lru_proj, w_out, norm_f_g):
    assert norm_g.shape[0] == 1, "single-layer trunk"
    wi = w_in[0]
    mt, wbt, wct, tab, wq = _prep_s5(s5_a_re[0], s5_a_im[0], s5_log_dt[0], s5_b_re[0], s5_b_im[0],
                                     s5_c_re[0], s5_c_im[0])
    tok = jnp.arange(PERM)
    w = {
        "perm": (tok[None, :] == SUB * (tok % SUB)[:, None] + (tok // SUB)[:, None]).astype(BF16),
        "norm_g": norm_g[0].reshape(1, D_MODEL).astype(F32),
        "w_s5_t": wi[:, :2 * D_MODEL].T.astype(BF16),
        "w_ulru": wi[:, 2 * D_MODEL:3 * D_MODEL].astype(BF16),
        "w_gs5": wi[:, 4 * D_MODEL:5 * D_MODEL].astype(BF16),
        "w_zg": jnp.concatenate([wi[:, 3 * D_MODEL:4 * D_MODEL], wi[:, 5 * D_MODEL:]], axis=1).astype(BF16),
        "s5_mt": mt, "s5_wbt": wbt, "s5_wct": wct, "s5_tab": tab, "s5_wq": wq,
        "s5_d_b": jnp.broadcast_to(s5_d[0].astype(F32)[:, None], (D_MODEL, LANES)),
        "glu_wt": s5_glu_w[0].T.astype(BF16),
        "glu_b_b": jnp.broadcast_to(s5_glu_b[0].astype(F32)[:, None], (D_MODEL, LANES)),
        "s5_proj": s5_proj[0].astype(BF16),
        "conv_w": lru_conv_w[0].astype(F32),
        "conv_b": lru_conv_b[0].reshape(1, D_MODEL).astype(F32),
        "wa": _pair_heads(lru_wa[0]),
        "wx": _pair_heads(lru_wx[0]),
        "ba": lru_ba[0].reshape(2, LRU_HEADS // 2, 1, LRU_PAIR).astype(F32),
        "bx": lru_bx[0].reshape(2, LRU_HEADS // 2, 1, LRU_PAIR).astype(F32),
        "cvec": (-RG_C * jax.nn.softplus(-lru_lambda[0].astype(F32))).reshape(2, LRU_HEADS // 2, 1, LRU_PAIR),
        "lru_proj": lru_proj[0].astype(BF16),
        "w_out": w_out[0].astype(BF16),
        "norm_f_g": norm_f_g.reshape(1, D_MODEL).astype(F32),
    }
    return (_trunk(x_prompt, w), _trunk(x_sample, w))
```

```python
import functools

import jax
import jax.numpy as jnp
from jax import lax
from jax.experimental import pallas as pl
from jax.experimental.pallas import tpu as pltpu

F32 = jnp.float32
BF16 = jnp.bfloat16

D_MODEL = 1024
SUB = 16
PERM = SUB * SUB
S5_GROUPS = 64
S5_GROUP_SIZE = 16
S5_STATE = 64
GROUPS_PER_STEP = 16
LRU_HEADS = 8
LRU_BLOCK = 128
LRU_PAIR = 2 * LRU_BLOCK
RG_C = 8.0
EPS = 1e-6
LANES = 128
SUBLANES = 8
Q_S5 = 128
R_BLK = 4
Q_IN = 64
Q_LRU_B = 64
Q_OUT = 32
VMEM_LIMIT = 56 * 1024 * 1024

T_STEP = 0
T_BLK = 24
T_FIX = 88
T_A16 = 96
T_AQ = 104
T_ROWS = 112


def _rms(x, g):
    return x * lax.rsqrt(jnp.mean(x * x, axis=-1, keepdims=True) + EPS) * g


def _class_major(h, perm):
    return jnp.dot(perm, h, preferred_element_type=F32).astype(BF16)


def _in_t_kernel(x_ref, g_ref, perm_ref, w_ref, o_ref, h_ref):
    @pl.when(pl.program_id(2) == 0)
    def _():
        _class_major_rows(x_ref, g_ref, perm_ref, h_ref, Q_S5)

    rows = R_BLK * Q_S5
    for rb in range(SUB // R_BLK):
        res = lax.dot_general(w_ref[...], h_ref[rb * rows:(rb + 1) * rows, :],
                              (((1,), (1,)), ((), ())), preferred_element_type=F32)
        for rr in range(R_BLK):
            o_ref[0, rb * R_BLK + rr] = res[:, rr * Q_S5:(rr + 1) * Q_S5]


def _in_n_kernel(x_ref, g_ref, perm_ref, w_ref, o_ref):
    for gi in range(Q_IN // SUB):
        xg = x_ref[0, gi * PERM:(gi + 1) * PERM, :]
        h = _class_major(_rms(xg, g_ref[...]).astype(BF16), perm_ref[...])
        res = jnp.dot(h, w_ref[...], preferred_element_type=F32)
        for r in range(SUB):
            o_ref[0, gi * SUB:(gi + 1) * SUB, r * D_MODEL:(r + 1) * D_MODEL] = res[r * SUB:(r + 1) * SUB, :]


def _s5_kernel(u_ref, d_ref, mt_ref, wbt_ref, wct_ref, tab_ref, wq_ref, o_ref,
               s_ref, cf_ref, cn_ref, cb_ref, hb_ref, *, nt):
    ph = pl.program_id(2)
    i = pl.program_id(3)
    tile = jnp.where(ph == 0, nt - 1 - i, i)
    q = Q_S5
    nblk = q // SUBLANES

    @pl.when(i == 0)
    def _():
        cf_ref[...] = jnp.zeros_like(cf_ref)
        cb_ref[...] = jnp.zeros_like(cb_ref)

    npair = GROUPS_PER_STEP // 2
    st = S5_STATE
    f_re, f_im, b_re, b_im = (slice(k * LANES, (k + 1) * LANES) for k in range(4))
    re, im = slice(0, LANES), slice(LANES, 2 * LANES)

    def cmul(cre, cim, xre, xim):
        return cre * xre - cim * xim, cre * xim + cim * xre

    def load_x(gl):
        c0 = gl * S5_GROUP_SIZE
        xt = u_ref[0, :, c0:c0 + S5_GROUP_SIZE, :]
        return c0, xt, xt.reshape(SUB * S5_GROUP_SIZE, q).astype(BF16)

    def plane(parts, k):
        return jnp.concatenate([p[k * st:(k + 1) * st, :] for p in parts], axis=0).T

    @pl.when(ph == 0)
    def _():
        for m in range(npair):
            cb = cb_ref[m]
            parts = []
            for gl in (2 * m, 2 * m + 1):
                wb = jnp.concatenate([wbt_ref[gl, st:2 * st, :], wbt_ref[gl, 3 * st:4 * st, :]], axis=0)
                parts.append(jnp.dot(wb, load_x(gl)[2], preferred_element_type=F32))
            hb_ref[tile, m] = cb
            w = wq_ref[m]
            pre, pim = cmul(w[:, re], w[:, im], plane(parts, 0), plane(parts, 1))
            red = jnp.concatenate([jnp.sum(pre, axis=0, keepdims=True),
                                   jnp.sum(pim, axis=0, keepdims=True)], axis=1)
            aq = tab_ref[m, T_AQ:T_AQ + SUBLANES, :]
            cre, cim = cmul(aq[:, b_re], aq[:, b_im], cb[:, re], cb[:, im])
            cb_ref[m] = (jnp.broadcast_to(red, (SUBLANES, 2 * LANES))
                         + jnp.concatenate([cre, cim], axis=1))

    @pl.when(ph == 1)
    def _():
        row8 = lax.broadcasted_iota(jnp.int32, (SUBLANES, LANES), 0)
        row16 = lax.broadcasted_iota(jnp.int32, (nblk, 2 * LANES), 0)
        rowq = lax.broadcasted_iota(jnp.int32, (q, 2 * LANES), 0)


        def stage_states(m):
            parts = []
            for gl in (2 * m, 2 * m + 1):
                c0, xt, xb = load_x(gl)
                parts.append(jnp.dot(wbt_ref[gl], xb, preferred_element_type=F32))
                y = jnp.dot(mt_ref[gl], xb, preferred_element_type=F32)
                dv = d_ref[c0:c0 + S5_GROUP_SIZE, :]
                o_ref[0, :, c0:c0 + S5_GROUP_SIZE, :] = (
                    y.reshape(SUB, S5_GROUP_SIZE, q) + xt * dv[None])
            for k, dst in enumerate((f_re, b_re, f_im, b_im)):
                s_ref[m, :, dst] = plane(parts, k)

        def stage_scan(m):
            sc = s_ref.at[m]
            cf_in, cb_in = cf_ref[m], hb_ref[tile, m]
            a16 = tab_ref[m, T_A16:T_A16 + SUBLANES, :]
            tre, tim = cmul(a16[:, f_re], a16[:, f_im], cf_in[:, re], cf_in[:, im])
            sc[0:SUBLANES, f_re] += jnp.where(row8 == 0, tre, 0.0)
            sc[0:SUBLANES, f_im] += jnp.where(row8 == 0, tim, 0.0)
            tre, tim = cmul(a16[:, b_re], a16[:, b_im], cb_in[:, re], cb_in[:, im])
            sc[q - SUBLANES:q, b_re] += jnp.where(row8 == SUBLANES - 1, tre, 0.0)
            sc[q - SUBLANES:q, b_im] += jnp.where(row8 == SUBLANES - 1, tim, 0.0)

            steps = [tab_ref[m, T_STEP + 8 * si:T_STEP + 8 * si + 8, :] for si in range(3)]
            ends_f, ends_b = [], []
            for k in range(nblk):
                rows = slice(k * SUBLANES, (k + 1) * SUBLANES)
                fr, fi, br, bi = sc[rows, f_re], sc[rows, f_im], sc[rows, b_re], sc[rows, b_im]
                for si, s in enumerate((1, 2, 4)):
                    c = steps[si]
                    dr, di = cmul(c[:, f_re], c[:, f_im], pltpu.roll(fr, s, 0), pltpu.roll(fi, s, 0))
                    fr, fi = fr + dr, fi + di
                    dr, di = cmul(c[:, b_re], c[:, b_im],
                                  pltpu.roll(br, SUBLANES - s, 0), pltpu.roll(bi, SUBLANES - s, 0))
                    br, bi = br + dr, bi + di
                sc[rows, :] = jnp.concatenate([fr, fi, br, bi], axis=1)
                ends_f.append(jnp.concatenate([fr[SUBLANES - 1:], fi[SUBLANES - 1:]], axis=1))
                ends_b.append(jnp.concatenate([br[0:1], bi[0:1]], axis=1))

            ef = jnp.concatenate(ends_f, axis=0)
            eb = jnp.concatenate(ends_b, axis=0)
            for si, s in enumerate((1, 2, 4, 8)):
                c = tab_ref[m, T_BLK + 16 * si:T_BLK + 16 * si + 16, :]
                dr, di = cmul(c[:, f_re], c[:, f_im], pltpu.roll(ef[:, re], s, 0), pltpu.roll(ef[:, im], s, 0))
                ef = ef + jnp.concatenate([dr, di], axis=1)
                dr, di = cmul(c[:, b_re], c[:, b_im],
                              pltpu.roll(eb[:, re], nblk - s, 0), pltpu.roll(eb[:, im], nblk - s, 0))
                eb = eb + jnp.concatenate([dr, di], axis=1)
            pf = jnp.where(row16 >= 1, pltpu.roll(ef, 1, 0), 0.0)
            pb = jnp.where(row16 <= nblk - 2, pltpu.roll(eb, nblk - 1, 0), 0.0)
            fix = tab_ref[m, T_FIX:T_FIX + SUBLANES, :]
            for k in range(nblk):
                rows = slice(k * SUBLANES, (k + 1) * SUBLANES)
                vf = jnp.broadcast_to(pf[k:k + 1, :], (SUBLANES, 2 * LANES))
                vb = jnp.broadcast_to(pb[k:k + 1, :], (SUBLANES, 2 * LANES))
                fr, fi = cmul(fix[:, f_re], fix[:, f_im], vf[:, re], vf[:, im])
                br, bi = cmul(fix[:, b_re], fix[:, b_im], vb[:, re], vb[:, im])
                sc[rows, :] += jnp.concatenate([fr, fi, br, bi], axis=1)
            cn_ref[m] = jnp.broadcast_to(ef[nblk - 1:nblk, :], (SUBLANES, 2 * LANES))

        def stage_out(m):
            h = s_ref[m]
            dn = jnp.where(rowq == 0, jnp.broadcast_to(cf_ref[m, 0:1, :], (q, 2 * LANES)),
                           pltpu.roll(h[:, :2 * LANES], 1, 0))
            up = jnp.where(rowq == q - 1, jnp.broadcast_to(hb_ref[tile, m, 0:1, :], (q, 2 * LANES)),
                           pltpu.roll(h[:, 2 * LANES:], q - 1, 0))
            hprev = jnp.concatenate([dn, up], axis=1).astype(BF16)
            y = lax.dot_general(wct_ref[m], hprev, (((1,), (1,)), ((), ())),
                                preferred_element_type=F32)
            for e in range(2):
                c0 = (2 * m + e) * S5_GROUP_SIZE
                o_ref[0, :, c0:c0 + S5_GROUP_SIZE, :] += (
                    y[e * 256:(e + 1) * 256, :].reshape(SUB, S5_GROUP_SIZE, q))

        for step in range(npair + 2):
            if step < npair:
                stage_states(step)
            if 0 <= step - 1 < npair:
                stage_scan(step - 1)
            if 0 <= step - 2 < npair:
                stage_out(step - 2)
        cf_ref[...] = cn_ref[...]


def _s5_post_kernel(y_ref, z_ref, g_ref, wg_ref, bg_ref, wp_ref, o_ref):
    ys = [jax.nn.gelu(y_ref[0, rr]) for rr in range(R_BLK)]
    y = jnp.concatenate(ys, axis=1)
    glu = jnp.dot(wg_ref[...], y.astype(BF16), preferred_element_type=F32)
    bias = jnp.concatenate([bg_ref[...]] * R_BLK, axis=1)
    y = y * jax.nn.sigmoid(glu + bias)
    z = jnp.concatenate([z_ref[0, rr] for rr in range(R_BLK)], axis=1)
    y = (y * jax.nn.silu(z)).astype(BF16)
    ys5 = lax.dot_general(y, wp_ref[...], (((0,), (0,)), ((), ())),
                          preferred_element_type=F32)
    for rr in range(R_BLK):
        g = g_ref[0, :, rr * D_MODEL:(rr + 1) * D_MODEL]
        o_ref[0, :, rr * D_MODEL:(rr + 1) * D_MODEL] = (
            jax.nn.sigmoid(g) * ys5[rr * Q_S5:(rr + 1) * Q_S5, :])


def _pair_cols(hp, r):
    return slice(r * D_MODEL + hp * LRU_PAIR, r * D_MODEL + (hp + 1) * LRU_PAIR)


def _lru_gates(xc_ref, hp, wa_ref, ba_ref, wx_ref, bx_ref, cv_ref, nq):
    xc = jnp.concatenate([xc_ref[0, :, _pair_cols(hp, r)] for r in range(SUB)], axis=0)
    xb = xc.astype(BF16)

    ra = jnp.dot(xb, wa_ref[hp], preferred_element_type=F32) + ba_ref[hp]
    rx = jnp.dot(xb, wx_ref[hp], preferred_element_type=F32) + bx_ref[hp]
    log_a = cv_ref[hp] * jax.nn.sigmoid(ra)
    a = jnp.exp(log_a)
    b = jnp.sqrt(1.0 - a * a) * (jax.nn.sigmoid(rx) * xc)
    return a, b


def _lru_scan(a, b, h_ref, col_of, p_ref, carry_row, nq, reverse):
    order = list(range(SUB - 1, -1, -1)) if reverse else list(range(SUB))
    h = None
    p = None
    for r in order:
        ar = a[r * nq:(r + 1) * nq, :]
        br = b[r * nq:(r + 1) * nq, :]
        h = br if h is None else ar * h + br
        p = ar if p is None else ar * p
        h_ref[0, :, col_of(r)] = h
        p_ref[r] = p

    row = lax.broadcasted_iota(jnp.int32, (nq, LRU_PAIR), 0)
    s = 1
    while s < nq:
        keep = (row < nq - s) if reverse else (row >= s)
        shift = (nq - s) if reverse else s
        hs = jnp.where(keep, pltpu.roll(h, shift, 0), 0.0)
        ps = jnp.where(keep, pltpu.roll(p, shift, 0), 1.0)
        h = h + p * hs
        p = p * ps
        s *= 2
    e = h + p * carry_row
    if reverse:
        cin = jnp.where(row < nq - 1, pltpu.roll(e, nq - 1, 0), carry_row)
        carry_out = e[0:1, :]
    else:
        cin = jnp.where(row >= 1, pltpu.roll(e, 1, 0), carry_row)
        carry_out = e[nq - 1:nq, :]
    for r in order:
        h_ref[0, :, col_of(r)] = h_ref[0, :, col_of(r)] + p_ref[r] * cin
    return carry_out


def _class_major_rows(x_ref, g_ref, perm_ref, h_ref, nq):
    for gi in range(nq // SUB):
        xg = x_ref[0, gi * PERM:(gi + 1) * PERM, :]
        hp = _class_major(_rms(xg, g_ref[...]).astype(BF16), perm_ref[...])
        for r in range(SUB):
            h_ref[r * nq + gi * SUB:r * nq + (gi + 1) * SUB, :] = hp[r * SUB:(r + 1) * SUB, :]


def _lru_bwd_kernel(x_ref, xh_ref, g_ref, perm_ref, wu_ref, cw_ref, cbias_ref,
                    wa_ref, ba_ref, wx_ref, bx_ref, cv_ref,
                    xc_ref, h_ref, u_ref, uh_ref, hn_ref, p_ref, nx_ref, carry_ref, *, nt):
    i = pl.program_id(1)
    tile = nt - 1 - i
    nq = Q_LRU_B

    @pl.when(i == 0)
    def _():
        nx_ref[...] = jnp.zeros_like(nx_ref)
        carry_ref[...] = jnp.zeros_like(carry_ref)

    _class_major_rows(x_ref, g_ref, perm_ref, hn_ref, nq)
    hh = _rms(xh_ref[0], g_ref[...]).astype(BF16)
    live = (tile > 0).astype(F32)
    row = lax.broadcasted_iota(jnp.int32, (nq, LRU_PAIR), 0)
    npairs = LRU_HEADS // 2

    def project(hp):
        cols = slice(hp * LRU_PAIR, (hp + 1) * LRU_PAIR)
        for rb in range(SUB // R_BLK):
            res = jnp.dot(hn_ref[rb * R_BLK * nq:(rb + 1) * R_BLK * nq, :], wu_ref[:, cols],
                          preferred_element_type=F32)
            for rr in range(R_BLK):
                u_ref[0, :, _pair_cols(hp, rb * R_BLK + rr)] = res[rr * nq:(rr + 1) * nq, :]
        uh_ref[hp] = jnp.dot(hh, wu_ref[:, cols], preferred_element_type=F32) * live

    def work(hp):
        cols = slice(hp * LRU_PAIR, (hp + 1) * LRU_PAIR)

        def cls(r):
            return u_ref[0, :, _pair_cols(hp, r)]
        uh = uh_ref[hp]
        taps = {-2: jnp.where(row == 0, uh[14:15, :], pltpu.roll(cls(14), 1, 0)),
                -1: jnp.where(row == 0, uh[15:16, :], pltpu.roll(cls(15), 1, 0)),
                SUB: jnp.where(row == nq - 1, nx_ref[0:1, cols], pltpu.roll(cls(0), nq - 1, 0))}
        nx_ref[0:1, cols] = u_ref[0, 0:1, _pair_cols(hp, 0)]

        def tap(r):
            return taps[r] if r in taps else cls(r)
        w = cw_ref[:, cols]
        for r in range(SUB):
            xc_ref[0, :, _pair_cols(hp, r)] = (
                cbias_ref[:, cols] + tap(r - 2) * w[0:1] + tap(r - 1) * w[1:2]
                + tap(r) * w[2:3] + tap(r + 1) * w[3:4])
        a, b = _lru_gates(xc_ref, hp, wa_ref, ba_ref, wx_ref, bx_ref, cv_ref, nq)
        carry_ref[0:1, cols] = _lru_scan(a, b, h_ref, functools.partial(_pair_cols, hp), p_ref.at[hp % 2],
                                         carry_ref[0:1, cols], nq, True)

    project(0)
    for hp in range(npairs):
        if hp + 1 < npairs:
            project(hp + 1)
        work(hp)


def _out_kernel(x_ref, xc_ref, hb_ref, m_ref, ng_ref, wzg_ref, wa_ref, ba_ref, wx_ref, bx_ref, cv_ref,
                wl_ref, wo_ref, gf_ref, perm_ref, o_ref, hf_ref, p_ref, carry_ref, v_ref, hn_ref, zl_ref,
                gl_ref):
    i = pl.program_id(1)
    nq = Q_OUT

    @pl.when(i == 0)
    def _():
        carry_ref[...] = jnp.zeros_like(carry_ref)

    _class_major_rows(x_ref, ng_ref, perm_ref, hn_ref, nq)

    npairs = LRU_HEADS // 2

    def project(hp):
        cols = slice(hp * LRU_PAIR, (hp + 1) * LRU_PAIR)
        gcols = slice(D_MODEL + hp * LRU_PAIR, D_MODEL + (hp + 1) * LRU_PAIR)
        zl_ref[:, cols] = jnp.dot(hn_ref[...], wzg_ref[:, cols], preferred_element_type=F32)
        gl_ref[:, cols] = jnp.dot(hn_ref[...], wzg_ref[:, gcols], preferred_element_type=F32)

    def work(hp):
        cols = slice(hp * LRU_PAIR, (hp + 1) * LRU_PAIR)
        a, b = _lru_gates(xc_ref, hp, wa_ref, ba_ref, wx_ref, bx_ref, cv_ref, nq)
        carry_ref[0:1, cols] = _lru_scan(a, b, hf_ref, functools.partial(_pair_cols, hp), p_ref.at[hp % 2],
                                         carry_ref[0:1, cols], nq, False)
        for r in range(SUB):
            hl = hf_ref[0, :, _pair_cols(hp, r)] + hb_ref[0, :, _pair_cols(hp, r)]
            v = hl * jax.nn.silu(zl_ref[r * nq:(r + 1) * nq, cols])
            v_ref[r * nq:(r + 1) * nq, cols] = v.astype(BF16)

    project(0)
    for hp in range(npairs):
        if hp + 1 < npairs:
            project(hp + 1)
        work(hp)

    ylru = jnp.dot(v_ref[...], wl_ref[...], preferred_element_type=F32)
    ms = []
    for r in range(SUB):
        sl = slice(r * D_MODEL, (r + 1) * D_MODEL)
        rows = slice(r * nq, (r + 1) * nq)
        ms.append((m_ref[0, :, sl] + jax.nn.sigmoid(gl_ref[rows, :]) * ylru[rows, :]).astype(BF16))
    for gi in range(nq // SUB):
        mg = jnp.concatenate([m[gi * SUB:(gi + 1) * SUB, :] for m in ms], axis=0)
        mn = _class_major(mg, perm_ref[...])
        dm = jnp.dot(mn, wo_ref[...], preferred_element_type=F32)
        xo = x_ref[0, gi * PERM:(gi + 1) * PERM, :] + dm
        o_ref[0, gi * PERM:(gi + 1) * PERM, :] = _rms(xo, gf_ref[...])


def _cx_mul(a, b):
    return a[0] * b[0] - a[1] * b[1], a[0] * b[1] + a[1] * b[0]


def _prep_s5(a_re, a_im, log_dt, b_re, b_im, c_re, c_im):
    hi = lax.Precision.HIGHEST
    g = S5_GROUPS
    are, aim = a_re.astype(F32), a_im.astype(F32)
    dt = jnp.exp(log_dt.astype(F32))[..., None]
    lre, lim = are * dt, aim * dt

    def apow(n):
        nn = n.astype(F32)[None, None, :, None]
        mag = jnp.exp(lre[:, :, None, :] * nn)
        ang = lim[:, :, None, :] * nn
        return mag * jnp.cos(ang), mag * jnp.sin(ang)

    ab = (jnp.exp(lre) * jnp.cos(lim), jnp.exp(lre) * jnp.sin(lim))
    den = are * are + aim * aim
    quo = (((ab[0] - 1.0) * are + ab[1] * aim) / den, (ab[1] * are - (ab[0] - 1.0) * aim) / den)
    bbar = _cx_mul((quo[0][..., None], quo[1][..., None]), (b_re.astype(F32), b_im.astype(F32)))
    cmat = (c_re.astype(F32), c_im.astype(F32))

    pw = apow(jnp.arange(SUB + 1))
    cp = _cx_mul((cmat[0][:, :, None], cmat[1][:, :, None]),
                 (pw[0][:, :, :SUB, None, :], pw[1][:, :, :SUB, None, :]))
    kern = (jnp.einsum('dgkip,dgpj->dgkij', cp[0], bbar[0], precision=hi)
            - jnp.einsum('dgkip,dgpj->dgkij', cp[1], bbar[1], precision=hi))
    kf, kb = kern[0], kern[1]
    by_lag = jnp.concatenate([kf[:, :0:-1], kf[:, :1] + kb[:, :1], kb[:, 1:]], axis=1)
    m = jnp.stack([by_lag[:, SUB - 1 - rp:2 * SUB - 1 - rp] for rp in range(SUB)], axis=1)
    mt = m.transpose(0, 1, 3, 2, 4).reshape(g, 256, 256)

    def cat(zf, zb, sign=1.0):
        return jnp.concatenate([zf[0], zb[0], sign * zf[1], sign * zb[1]], axis=-1)

    def sel(z, d, fn):
        return fn(z[0][d]), fn(z[1][d])

    bt = [(bbar[0][d].transpose(0, 2, 1)[:, None], bbar[1][d].transpose(0, 2, 1)[:, None])
          for d in range(2)]
    wf = _cx_mul(sel(pw, 0, lambda z: z[:, ::-1][:, 1:, None, :]), bt[0])
    wb = _cx_mul(sel(pw, 1, lambda z: z[:, :SUB, None, :]), bt[1])
    wbt = cat(wf, wb).reshape(g, 256, 256).transpose(0, 2, 1)
    cf = _cx_mul(sel(cmat, 0, lambda z: z[:, None]), sel(pw, 0, lambda z: z[:, 1:, None, :]))
    cb = _cx_mul(sel(cmat, 1, lambda z: z[:, None]),
                 sel(pw, 1, lambda z: z[:, ::-1][:, :SUB, None, :]))
    planes = jnp.stack([cf[0], -cf[1], cb[0], -cb[1]], axis=3)
    planes = planes.reshape(g // 2, 2, 256, 4, S5_STATE)
    wct = jnp.einsum('aerkp,ef->aerkfp', planes, jnp.eye(2, dtype=F32)).reshape(g // 2, 512, 512)

    def pair_lanes(z):
        rows = z.shape[1]
        return z.reshape(g // 2, 2, rows, S5_STATE).transpose(0, 2, 1, 3).reshape(g // 2, rows, 2 * S5_STATE)

    def tab(nf, nb, mf, mb, backward_only=False):
        zf = sel(apow(nf), 0, lambda z: pair_lanes(z * mf[None, :, None]))
        zb = sel(apow(nb), 1, lambda z: pair_lanes(z * mb[None, :, None]))
        return jnp.concatenate(([] if backward_only else [zf[0], zf[1]]) + [zb[0], zb[1]], axis=-1)

    m8 = jnp.arange(SUBLANES)
    nblk = Q_S5 // SUBLANES
    m16 = jnp.arange(nblk)
    parts = []
    for s in (1, 2, 4):
        n = jnp.full((SUBLANES,), SUB * s)
        parts.append(tab(n, n, (m8 >= s).astype(F32), (m8 < SUBLANES - s).astype(F32)))
    for s in (1, 2, 4, 8):
        n = jnp.full((nblk,), SUB * SUBLANES * s)
        parts.append(tab(n, n, (m16 >= s).astype(F32), (m16 < nblk - s).astype(F32)))
    one8 = jnp.ones((SUBLANES,), F32)
    parts.append(tab(SUB * (m8 + 1), SUB * (SUBLANES - m8), one8, one8))
    for n in (SUB, SUB * Q_S5):
        nn = jnp.full((SUBLANES,), n)
        parts.append(tab(nn, nn, one8, one8))
    table = jnp.concatenate(parts, axis=1)
    nq = SUB * jnp.arange(Q_S5)
    wq = tab(nq, nq, jnp.ones((Q_S5,), F32), jnp.ones((Q_S5,), F32), backward_only=True)
    return mt.astype(BF16), wbt.astype(BF16), wct.astype(BF16), table, wq


def _pair_heads(w):
    w = w.astype(BF16).reshape(2, LRU_HEADS // 2, 2, LRU_BLOCK, LRU_BLOCK)
    z = jnp.zeros_like(w[:, :, 0])
    top = jnp.concatenate([w[:, :, 0], z], axis=-1)
    bot = jnp.concatenate([z, w[:, :, 1]], axis=-1)
    return jnp.concatenate([top, bot], axis=-2)


def _full(shape):
    return pl.BlockSpec(shape, lambda *_: (0,) * len(shape))


def _params(sem, **kw):
    return pltpu.CompilerParams(dimension_semantics=sem, vmem_limit_bytes=VMEM_LIMIT, **kw)


def _trunk(x, w):
    bsz, seq, _ = x.shape
    lq = seq // SUB
    assert seq % (SUB * Q_S5) == 0
    row_shape = jax.ShapeDtypeStruct((bsz, lq, SUB * D_MODEL), F32)

    nt = lq // Q_S5
    uz = pl.pallas_call(
        _in_t_kernel,
        grid=(bsz, nt, 2),
        in_specs=[pl.BlockSpec((1, SUB * Q_S5, D_MODEL), lambda b, t, c: (b, t, 0)),
                  _full((1, D_MODEL)), _full((PERM, PERM)),
                  pl.BlockSpec((D_MODEL, D_MODEL), lambda b, t, c: (c, 0))],
        out_specs=pl.BlockSpec((1, SUB, D_MODEL, Q_S5), lambda b, t, c: (b, 0, c, t)),
        out_shape=jax.ShapeDtypeStruct((bsz, SUB, 2 * D_MODEL, lq), F32),
        scratch_shapes=[pltpu.VMEM((SUB * Q_S5, D_MODEL), BF16)],
        compiler_params=_params(("parallel", "parallel", "arbitrary")),
        name="in_proj_t",
    )(x, w["norm_g"], w["perm"], w["w_s5_t"])

    rows = pl.BlockSpec((1, Q_IN, SUB * D_MODEL), lambda b, t: (b, t, 0))
    g_s5 = pl.pallas_call(
        _in_n_kernel,
        grid=(bsz, lq // Q_IN),
        in_specs=[pl.BlockSpec((1, SUB * Q_IN, D_MODEL), lambda b, t: (b, t, 0)),
                  _full((1, D_MODEL)), _full((PERM, PERM)), _full((D_MODEL, D_MODEL))],
        out_specs=rows,
        out_shape=row_shape,
        compiler_params=_params(("parallel", "parallel")),
        name="in_proj_n",
    )(x, w["norm_g"], w["perm"], w["w_gs5"])

    def tile_of(ph, i):
        return jnp.where(ph == 0, nt - 1 - i, i)
    npair = GROUPS_PER_STEP // 2
    slab = GROUPS_PER_STEP * S5_GROUP_SIZE
    gw = pl.BlockSpec((GROUPS_PER_STEP, 256, 256), lambda g, b, ph, i: (g, 0, 0))
    y_t = pl.pallas_call(
        functools.partial(_s5_kernel, nt=nt),
        grid=(S5_GROUPS // GROUPS_PER_STEP, bsz, 2, nt),
        in_specs=[pl.BlockSpec((1, SUB, slab, Q_S5), lambda g, b, ph, i: (b, 0, g, tile_of(ph, i))),
                  pl.BlockSpec((slab, LANES), lambda g, b, ph, i: (g, 0)),
                  gw, gw,
                  pl.BlockSpec((npair, 512, 512), lambda g, b, ph, i: (g, 0, 0)),
                  pl.BlockSpec((npair, T_ROWS, 4 * LANES), lambda g, b, ph, i: (g, 0, 0)),
                  pl.BlockSpec((npair, Q_S5, 2 * LANES), lambda g, b, ph, i: (g, 0, 0))],
        out_specs=pl.BlockSpec((1, SUB, slab, Q_S5), lambda g, b, ph, i: (b, 0, g, ph * i)),
        out_shape=jax.ShapeDtypeStruct((bsz, SUB, D_MODEL, lq), F32),
        scratch_shapes=[pltpu.VMEM((npair, Q_S5, 4 * LANES), F32),
                        pltpu.VMEM((npair, SUBLANES, 2 * LANES), F32),
                        pltpu.VMEM((npair, SUBLANES, 2 * LANES), F32),
                        pltpu.VMEM((npair, SUBLANES, 2 * LANES), F32),
                        pltpu.VMEM((nt, npair, SUBLANES, 2 * LANES), F32)],
        compiler_params=_params(("arbitrary",) * 4),
        name="s5_ssm",
    )(uz, w["s5_d_b"], w["s5_mt"], w["s5_wbt"], w["s5_wct"], w["s5_tab"], w["s5_wq"])

    m_s5 = pl.pallas_call(
        _s5_post_kernel,
        grid=(bsz, nt, SUB // R_BLK),
        in_specs=[pl.BlockSpec((1, R_BLK, D_MODEL, Q_S5), lambda b, t, r: (b, r, 0, t)),
                  pl.BlockSpec((1, R_BLK, D_MODEL, Q_S5), lambda b, t, r: (b, r, 1, t)),
                  pl.BlockSpec((1, Q_S5, R_BLK * D_MODEL), lambda b, t, r: (b, t, r)),
                  _full((D_MODEL, D_MODEL)), _full((D_MODEL, LANES)), _full((D_MODEL, D_MODEL))],
        out_specs=pl.BlockSpec((1, Q_S5, R_BLK * D_MODEL), lambda b, t, r: (b, t, r)),
        out_shape=row_shape,
        compiler_params=_params(("parallel", "parallel", "parallel")),
        name="s5_post",
    )(y_t, uz, g_s5, w["glu_wt"], w["glu_b_b"], w["s5_proj"])

    ntb = lq // Q_LRU_B
    rows_b = pl.BlockSpec((1, Q_LRU_B, SUB * D_MODEL), lambda b, i: (b, ntb - 1 - i, 0))
    gate_w = _full((LRU_HEADS // 2, LRU_PAIR, LRU_PAIR))
    gate_b = _full((LRU_HEADS // 2, 1, LRU_PAIR))
    xc, h_bwd = pl.pallas_call(
        functools.partial(_lru_bwd_kernel, nt=ntb),
        grid=(bsz, ntb),
        in_specs=[pl.BlockSpec((1, SUB * Q_LRU_B, D_MODEL), lambda b, i: (b, ntb - 1 - i, 0)),
                  pl.BlockSpec((1, SUB, D_MODEL),
                               lambda b, i: (b, jnp.maximum((ntb - 1 - i) * Q_LRU_B - 1, 0), 0)),
                  _full((1, D_MODEL)), _full((PERM, PERM)), _full((D_MODEL, D_MODEL)),
                  _full((4, D_MODEL)), _full((1, D_MODEL)),
                  gate_w, gate_b, gate_w, gate_b, gate_b],
        out_specs=[rows_b, rows_b],
        out_shape=[row_shape, row_shape],
        scratch_shapes=[pltpu.VMEM((1, Q_LRU_B, SUB * D_MODEL), F32),
                        pltpu.VMEM((LRU_HEADS // 2, SUB, LRU_PAIR), F32),
                        pltpu.VMEM((SUB * Q_LRU_B, D_MODEL), BF16),
                        pltpu.VMEM((2, SUB, Q_LRU_B, LRU_PAIR), F32),
                        pltpu.VMEM((SUBLANES, D_MODEL), F32),
                        pltpu.VMEM((SUBLANES, D_MODEL), F32)],
        compiler_params=_params(("arbitrary", "arbitrary")),
        name="lru_bwd",
    )(x, x, w["norm_g"], w["perm"], w["w_ulru"], w["conv_w"], w["conv_b"],
      w["wa"][1], w["ba"][1], w["wx"][1], w["bx"][1], w["cvec"][1])

    rows_o = pl.BlockSpec((1, Q_OUT, SUB * D_MODEL), lambda b, i: (b, i, 0))
    x_o = pl.BlockSpec((1, SUB * Q_OUT, D_MODEL), lambda b, i: (b, i, 0))
    out = pl.pallas_call(
        _out_kernel,
        grid=(bsz, lq // Q_OUT),
        in_specs=[x_o] + [rows_o] * 3 + [_full((1, D_MODEL)), _full((D_MODEL, 2 * D_MODEL)),
                                         gate_w, gate_b, gate_w, gate_b, gate_b,
                                         _full((D_MODEL, D_MODEL)), _full((D_MODEL, D_MODEL)),
                                         _full((1, D_MODEL)), _full((PERM, PERM))],
        out_specs=x_o,
        out_shape=jax.ShapeDtypeStruct((bsz, seq, D_MODEL), F32),
        scratch_shapes=[pltpu.VMEM((1, Q_OUT, SUB * D_MODEL), F32),
                        pltpu.VMEM((2, SUB, Q_OUT, LRU_PAIR), F32),
                        pltpu.VMEM((SUBLANES, D_MODEL), F32),
                        pltpu.VMEM((SUB * Q_OUT, D_MODEL), BF16),
                        pltpu.VMEM((SUB * Q_OUT, D_MODEL), BF16),
                        pltpu.VMEM((SUB * Q_OUT, D_MODEL), F32),
                        pltpu.VMEM((SUB * Q_OUT, D_MODEL), F32)],
        compiler_params=_params(("arbitrary", "arbitrary")),
        name="lru_fwd_out",
    )(x, xc, h_bwd, m_s5, w["norm_g"], w["w_zg"], w["wa"][0], w["ba"][0], w["wx"][0], w["bx"][0],
      w["cvec"][0], w["lru_proj"], w["w_out"], w["norm_f_g"], w["perm"])
    return out


def kernel(x_prompt, x_sample, norm_g, w_in, s5_a_re, s5_a_im, s5_log_dt, s5_b_re, s5_b_im, s5_c_re, s5_c_im, s5_d, s5_glu_w, s5_glu_b, s5_proj, lru_conv_w, lru_conv_b, lru_lambda, lru_wa, lru_ba, lru_wx, lru_bx, lru_proj, w_out, norm_f_g):
    assert norm_g.shape[0] == 1, "single-layer trunk"
    wi = w_in[0]
    mt, wbt, wct, tab, wq = _prep_s5(s5_a_re[0], s5_a_im[0], s5_log_dt[0], s5_b_re[0], s5_b_im[0],
                                     s5_c_re[0], s5_c_im[0])
    tok = jnp.arange(PERM)
    w = {
        "perm": (tok[None, :] == SUB * (tok % SUB)[:, None] + (tok // SUB)[:, None]).astype(BF16),
        "norm_g": norm_g[0].reshape(1, D_MODEL).astype(F32),
        "w_s5_t": wi[:, :2 * D_MODEL].T.astype(BF16),
        "w_ulru": wi[:, 2 * D_MODEL:3 * D_MODEL].astype(BF16),
        "w_gs5": wi[:, 4 * D_MODEL:5 * D_MODEL].astype(BF16),
        "w_zg": jnp.concatenate([wi[:, 3 * D_MODEL:4 * D_MODEL], wi[:, 5 * D_MODEL:]], axis=1).astype(BF16),
        "s5_mt": mt, "s5_wbt": wbt, "s5_wct": wct, "s5_tab": tab, "s5_wq": wq,
        "s5_d_b": jnp.broadcast_to(s5_d[0].astype(F32)[:, None], (D_MODEL, LANES)),
        "glu_wt": s5_glu_w[0].T.astype(BF16),
        "glu_b_b": jnp.broadcast_to(s5_glu_b[0].astype(F32)[:, None], (D_MODEL, LANES)),
        "s5_proj": s5_proj[0].astype(BF16),
        "conv_w": lru_conv_w[0].astype(F32),
        "conv_b": lru_conv_b[0].reshape(1, D_MODEL).astype(F32),
        "wa": _pair_heads(lru_wa[0]),
        "wx": _pair_heads(lru_wx[0]),
        "ba": lru_ba[0].reshape(2, LRU_HEADS // 2, 1, LRU_PAIR).astype(F32),
        "bx": lru_bx[0].reshape(2, LRU_HEADS // 2, 1, LRU_PAIR).astype(F32),
        "cvec": (-RG_C * jax.nn.softplus(-lru_lambda[0].astype(F32))).reshape(2, LRU_HEADS // 2, 1, LRU_PAIR),
        "lru_proj": lru_proj[0].astype(BF16),
        "w_out": w_out[0].astype(BF16),
        "norm_f_g": norm_f_g.reshape(1, D_MODEL).astype(F32),
    }
    return (_trunk(x_prompt, w), _trunk(x_sample, w))
```

```python
import functools

import jax
import jax.numpy as jnp
from jax import lax
from jax.experimental import pallas as pl
from jax.experimental.pallas import tpu as pltpu

F32 = jnp.float32
BF16 = jnp.bfloat16

D_MODEL = 1024
SUB = 16
PERM = SUB * SUB
S5_GROUPS = 64
S5_GROUP_SIZE = 16
S5_STATE = 64
GROUPS_PER_STEP = 16
LRU_HEADS = 8
LRU_BLOCK = 128
LRU_PAIR = 2 * LRU_BLOCK
RG_C = 8.0
EPS = 1e-6
LANES = 128
SUBLANES = 8
Q_S5 = 128
R_BLK = 4
Q_IN = 64
Q_LRU_B = 64
Q_OUT = 32
VMEM_LIMIT = 56 * 1024 * 1024

T_STEP = 0
T_BLK = 24
T_FIX = 88
T_A16 = 96
T_AQ = 104
T_ROWS = 112


def _rms(x, g):
    return x * lax.rsqrt(jnp.mean(x * x, axis=-1, keepdims=True) + EPS) * g


def _class_major(h, perm):
    return jnp.dot(perm, h, preferred_element_type=F32).astype(BF16)


def _in_t_kernel(x_ref, g_ref, perm_ref, w_ref, o_ref, h_ref):
    @pl.when(pl.program_id(2) == 0)
    def _():
        _class_major_rows(x_ref, g_ref, perm_ref, h_ref, Q_S5)

    rows = R_BLK * Q_S5
    for rb in range(SUB // R_BLK):
        res = lax.dot_general(w_ref[...], h_ref[rb * rows:(rb + 1) * rows, :],
                              (((1,), (1,)), ((), ())), preferred_element_type=F32)
        for rr in range(R_BLK):
            o_ref[0, rb * R_BLK + rr] = res[:, rr * Q_S5:(rr + 1) * Q_S5]


def _in_n_kernel(x_ref, g_ref, perm_ref, w_ref, o_ref):
    for gi in range(Q_IN // SUB):
        xg = x_ref[0, gi * PERM:(gi + 1) * PERM, :]
        h = _class_major(_rms(xg, g_ref[...]).astype(BF16), perm_ref[...])
        res = jnp.dot(h, w_ref[...], preferred_element_type=F32)
        for r in range(SUB):
            o_ref[0, gi * SUB:(gi + 1) * SUB, r * D_MODEL:(r + 1) * D_MODEL] = res[r * SUB:(r + 1) * SUB, :]


def _s5_kernel(u_ref, d_ref, mt_ref, wbt_ref, wct_ref, tab_ref, wq_ref, o_ref,
               s_ref, cf_ref, cn_ref, cb_ref, hb_ref, *, nt):
    ph = pl.program_id(2)
    i = pl.program_id(3)
    tile = jnp.where(ph == 0, nt - 1 - i, i)
    q = Q_S5
    nblk = q // SUBLANES

    @pl.when(i == 0)
    def _():
        cf_ref[...] = jnp.zeros_like(cf_ref)
        cb_ref[...] = jnp.zeros_like(cb_ref)

    npair = GROUPS_PER_STEP // 2
    st = S5_STATE
    f_re, f_im, b_re, b_im = (slice(k * LANES, (k + 1) * LANES) for k in range(4))
    re, im = slice(0, LANES), slice(LANES, 2 * LANES)

    def cmul(cre, cim, xre, xim):
        return cre * xre - cim * xim, cre * xim + cim * xre

    def load_x(gl):
        c0 = gl * S5_GROUP_SIZE
        xt = u_ref[0, :, c0:c0 + S5_GROUP_SIZE, :]
        return c0, xt, xt.reshape(SUB * S5_GROUP_SIZE, q).astype(BF16)

    def plane(parts, k):
        return jnp.concatenate([p[k * st:(k + 1) * st, :] for p in parts], axis=0).T

    @pl.when(ph == 0)
    def _():
        for m in range(npair):
            cb = cb_ref[m]
            parts = []
            for gl in (2 * m, 2 * m + 1):
                wb = jnp.concatenate([wbt_ref[gl, st:2 * st, :], wbt_ref[gl, 3 * st:4 * st, :]], axis=0)
                parts.append(jnp.dot(wb, load_x(gl)[2], preferred_element_type=F32))
            hb_ref[tile, m] = cb
            w = wq_ref[m]
            pre, pim = cmul(w[:, re], w[:, im], plane(parts, 0), plane(parts, 1))
            red = jnp.concatenate([jnp.sum(pre, axis=0, keepdims=True),
                                   jnp.sum(pim, axis=0, keepdims=True)], axis=1)
            aq = tab_ref[m, T_AQ:T_AQ + SUBLANES, :]
            cre, cim = cmul(aq[:, b_re], aq[:, b_im], cb[:, re], cb[:, im])
            cb_ref[m] = (jnp.broadcast_to(red, (SUBLANES, 2 * LANES))
                         + jnp.concatenate([cre, cim], axis=1))

    @pl.when(ph == 1)
    def _():
        row8 = lax.broadcasted_iota(jnp.int32, (SUBLANES, LANES), 0)
        row16 = lax.broadcasted_iota(jnp.int32, (nblk, 2 * LANES), 0)
        rowq = lax.broadcasted_iota(jnp.int32, (q, 2 * LANES), 0)


        def stage_states(m):
            parts = []
            for gl in (2 * m, 2 * m + 1):
                c0, xt, xb = load_x(gl)
                parts.append(jnp.dot(wbt_ref[gl], xb, preferred_element_type=F32))
                y = jnp.dot(mt_ref[gl], xb, preferred_element_type=F32)
                dv = d_ref[c0:c0 + S5_GROUP_SIZE, :]
                o_ref[0, :, c0:c0 + S5_GROUP_SIZE, :] = (
                    y.reshape(SUB, S5_GROUP_SIZE, q) + xt * dv[None])
            for k, dst in enumerate((f_re, b_re, f_im, b_im)):
                s_ref[m, :, dst] = plane(parts, k)

        def stage_scan(m):
            sc = s_ref.at[m]
            cf_in, cb_in = cf_ref[m], hb_ref[tile, m]
            a16 = tab_ref[m, T_A16:T_A16 + SUBLANES, :]
            tre, tim = cmul(a16[:, f_re], a16[:, f_im], cf_in[:, re], cf_in[:, im])
            sc[0:SUBLANES, f_re] += jnp.where(row8 == 0, tre, 0.0)
            sc[0:SUBLANES, f_im] += jnp.where(row8 == 0, tim, 0.0)
            tre, tim = cmul(a16[:, b_re], a16[:, b_im], cb_in[:, re], cb_in[:, im])
            sc[q - SUBLANES:q, b_re] += jnp.where(row8 == SUBLANES - 1, tre, 0.0)
            sc[q - SUBLANES:q, b_im] += jnp.where(row8 == SUBLANES - 1, tim, 0.0)

            steps = [tab_ref[m, T_STEP + 8 * si:T_STEP + 8 * si + 8, :] for si in range(3)]
            ends_f, ends_b = [], []
            for k in range(nblk):
                rows = slice(k * SUBLANES, (k + 1) * SUBLANES)
                fr, fi, br, bi = sc[rows, f_re], sc[rows, f_im], sc[rows, b_re], sc[rows, b_im]
                for si, s in enumerate((1, 2, 4)):
                    c = steps[si]
                    dr, di = cmul(c[:, f_re], c[:, f_im], pltpu.roll(fr, s, 0), pltpu.roll(fi, s, 0))
                    fr, fi = fr + dr, fi + di
                    dr, di = cmul(c[:, b_re], c[:, b_im],
                                  pltpu.roll(br, SUBLANES - s, 0), pltpu.roll(bi, SUBLANES - s, 0))
                    br, bi = br + dr, bi + di
                sc[rows, :] = jnp.concatenate([fr, fi, br, bi], axis=1)
                ends_f.append(jnp.concatenate([fr[SUBLANES - 1:], fi[SUBLANES - 1:]], axis=1))
                ends_b.append(jnp.concatenate([br[0:1], bi[0:1]], axis=1))

            ef = jnp.concatenate(ends_f, axis=0)
            eb = jnp.concatenate(ends_b, axis=0)
            for si, s in enumerate((1, 2, 4, 8)):
                c = tab_ref[m, T_BLK + 16 * si:T_BLK + 16 * si + 16, :]
                dr, di = cmul(c[:, f_re], c[:, f_im], pltpu.roll(ef[:, re], s, 0), pltpu.roll(ef[:, im], s, 0))
                ef = ef + jnp.concatenate([dr, di], axis=1)
                dr, di = cmul(c[:, b_re], c[:, b_im],
                              pltpu.roll(eb[:, re], nblk - s, 0), pltpu.roll(eb[:, im], nblk - s, 0))
                eb = eb + jnp.concatenate([dr, di], axis=1)
            pf = jnp.where(row16 >= 1, pltpu.roll(ef, 1, 0), 0.0)
            pb = jnp.where(row16 <= nblk - 2, pltpu.roll(eb, nblk - 1, 0), 0.0)
            fix = tab_ref[m, T_FIX:T_FIX + SUBLANES, :]
            for k in range(nblk):
                rows = slice(k * SUBLANES, (k + 1) * SUBLANES)
                vf = jnp.broadcast_to(pf[k:k + 1, :], (SUBLANES, 2 * LANES))
                vb = jnp.broadcast_to(pb[k:k + 1, :], (SUBLANES, 2 * LANES))
                fr, fi = cmul(fix[:, f_re], fix[:, f_im], vf[:, re], vf[:, im])
                br, bi = cmul(fix[:, b_re], fix[:, b_im], vb[:, re], vb[:, im])
                sc[rows, :] += jnp.concatenate([fr, fi, br, bi], axis=1)
            cn_ref[m] = jnp.broadcast_to(ef[nblk - 1:nblk, :], (SUBLANES, 2 * LANES))

        def stage_out(m):
            h = s_ref[m]
            dn = jnp.where(rowq == 0, jnp.broadcast_to(cf_ref[m, 0:1, :], (q, 2 * LANES)),
                           pltpu.roll(h[:, :2 * LANES], 1, 0))
            up = jnp.where(rowq == q - 1, jnp.broadcast_to(hb_ref[tile, m, 0:1, :], (q, 2 * LANES)),
                           pltpu.roll(h[:, 2 * LANES:], q - 1, 0))
            hprev = jnp.concatenate([dn, up], axis=1).astype(BF16)
            y = lax.dot_general(wct_ref[m], hprev, (((1,), (1,)), ((), ())),
                                preferred_element_type=F32)
            for e in range(2):
                c0 = (2 * m + e) * S5_GROUP_SIZE
                o_ref[0, :, c0:c0 + S5_GROUP_SIZE, :] += (
                    y[e * 256:(e + 1) * 256, :].reshape(SUB, S5_GROUP_SIZE, q))

        for step in range(npair + 2):
            if step < npair:
                stage_states(step)
            if 0 <= step - 1 < npair:
                stage_scan(step - 1)
            if 0 <= step - 2 < npair:
                stage_out(step - 2)
        cf_ref[...] = cn_ref[...]


def _s5_post_kernel(y_ref, z_ref, g_ref, wg_ref, bg_ref, wp_ref, o_ref):
    ys = [jax.nn.gelu(y_ref[0, rr]) for rr in range(R_BLK)]
    y = jnp.concatenate(ys, axis=1)
    glu = jnp.dot(wg_ref[...], y.astype(BF16), preferred_element_type=F32)
    bias = jnp.concatenate([bg_ref[...]] * R_BLK, axis=1)
    y = y * jax.nn.sigmoid(glu + bias)
    z = jnp.concatenate([z_ref[0, rr] for rr in range(R_BLK)], axis=1)
    y = (y * jax.nn.silu(z)).astype(BF16)
    ys5 = lax.dot_general(y, wp_ref[...], (((0,), (0,)), ((), ())),
                          preferred_element_type=F32)
    for rr in range(R_BLK):
        g = g_ref[0, :, rr * D_MODEL:(rr + 1) * D_MODEL]
        o_ref[0, :, rr * D_MODEL:(rr + 1) * D_MODEL] = (
            jax.nn.sigmoid(g) * ys5[rr * Q_S5:(rr + 1) * Q_S5, :])


def _pair_cols(hp, r):
    return slice(r * D_MODEL + hp * LRU_PAIR, r * D_MODEL + (hp + 1) * LRU_PAIR)


def _lru_gates(xc_ref, hp, wa_ref, ba_ref, wx_ref, bx_ref, cv_ref, nq):
    xc = jnp.concatenate([xc_ref[0, :, _pair_cols(hp, r)] for r in range(SUB)], axis=0)
    xb = xc.astype(BF16)

    ra = jnp.dot(xb, wa_ref[hp], preferred_element_type=F32) + ba_ref[hp]
    rx = jnp.dot(xb, wx_ref[hp], preferred_element_type=F32) + bx_ref[hp]
    log_a = cv_ref[hp] * jax.nn.sigmoid(ra)
    a = jnp.exp(log_a)
    gap = 1.0 - a * a
    root = jnp.where(gap > 0.0, gap * lax.rsqrt(gap), 0.0)
    b = root * (jax.nn.sigmoid(rx) * xc)
    return a, b


def _lru_scan(a, b, h_ref, col_of, p_ref, carry_row, nq, reverse):
    order = list(range(SUB - 1, -1, -1)) if reverse else list(range(SUB))
    h = None
    p = None
    for r in order:
        ar = a[r * nq:(r + 1) * nq, :]
        br = b[r * nq:(r + 1) * nq, :]
        h = br if h is None else ar * h + br
        p = ar if p is None else ar * p
        h_ref[0, :, col_of(r)] = h
        p_ref[r] = p

    row = lax.broadcasted_iota(jnp.int32, (nq, LRU_PAIR), 0)
    s = 1
    while s < nq:
        keep = (row < nq - s) if reverse else (row >= s)
        shift = (nq - s) if reverse else s
        hs = jnp.where(keep, pltpu.roll(h, shift, 0), 0.0)
        ps = jnp.where(keep, pltpu.roll(p, shift, 0), 1.0)
        h = h + p * hs
        p = p * ps
        s *= 2
    e = h + p * carry_row
    if reverse:
        cin = jnp.where(row < nq - 1, pltpu.roll(e, nq - 1, 0), carry_row)
        carry_out = e[0:1, :]
    else:
        cin = jnp.where(row >= 1, pltpu.roll(e, 1, 0), carry_row)
        carry_out = e[nq - 1:nq, :]
    for r in order:
        h_ref[0, :, col_of(r)] = h_ref[0, :, col_of(r)] + p_ref[r] * cin
    return carry_out


def _class_major_rows(x_ref, g_ref, perm_ref, h_ref, nq):
    for gi in range(nq // SUB):
        xg = x_ref[0, gi * PERM:(gi + 1) * PERM, :]
        hp = _class_major(_rms(xg, g_ref[...]).astype(BF16), perm_ref[...])
        for r in range(SUB):
            h_ref[r * nq + gi * SUB:r * nq + (gi + 1) * SUB, :] = hp[r * SUB:(r + 1) * SUB, :]


def _lru_bwd_kernel(x_ref, xh_ref, g_ref, perm_ref, wu_ref, cw_ref, cbias_ref,
                    wa_ref, ba_ref, wx_ref, bx_ref, cv_ref,
                    xc_ref, h_ref, u_ref, uh_ref, hn_ref, p_ref, nx_ref, carry_ref, *, nt):
    i = pl.program_id(1)
    tile = nt - 1 - i
    nq = Q_LRU_B

    @pl.when(i == 0)
    def _():
        nx_ref[...] = jnp.zeros_like(nx_ref)
        carry_ref[...] = jnp.zeros_like(carry_ref)

    _class_major_rows(x_ref, g_ref, perm_ref, hn_ref, nq)
    hh = _rms(xh_ref[0], g_ref[...]).astype(BF16)
    live = (tile > 0).astype(F32)
    row = lax.broadcasted_iota(jnp.int32, (nq, LRU_PAIR), 0)
    npairs = LRU_HEADS // 2

    def project(hp):
        cols = slice(hp * LRU_PAIR, (hp + 1) * LRU_PAIR)
        for rb in range(SUB // R_BLK):
            res = jnp.dot(hn_ref[rb * R_BLK * nq:(rb + 1) * R_BLK * nq, :], wu_ref[:, cols],
                          preferred_element_type=F32)
            for rr in range(R_BLK):
                u_ref[0, :, _pair_cols(hp, rb * R_BLK + rr)] = res[rr * nq:(rr + 1) * nq, :]
        uh_ref[hp] = jnp.dot(hh, wu_ref[:, cols], preferred_element_type=F32) * live

    def work(hp):
        cols = slice(hp * LRU_PAIR, (hp + 1) * LRU_PAIR)

        def cls(r):
            return u_ref[0, :, _pair_cols(hp, r)]
        uh = uh_ref[hp]
        taps = {-2: jnp.where(row == 0, uh[14:15, :], pltpu.roll(cls(14), 1, 0)),
                -1: jnp.where(row == 0, uh[15:16, :], pltpu.roll(cls(15), 1, 0)),
                SUB: jnp.where(row == nq - 1, nx_ref[0:1, cols], pltpu.roll(cls(0), nq - 1, 0))}
        nx_ref[0:1, cols] = u_ref[0, 0:1, _pair_cols(hp, 0)]

        def tap(r):
            return taps[r] if r in taps else cls(r)
        w = cw_ref[:, cols]
        for r in range(SUB):
            xc_ref[0, :, _pair_cols(hp, r)] = (
                cbias_ref[:, cols] + tap(r - 2) * w[0:1] + tap(r - 1) * w[1:2]
                + tap(r) * w[2:3] + tap(r + 1) * w[3:4])
        a, b = _lru_gates(xc_ref, hp, wa_ref, ba_ref, wx_ref, bx_ref, cv_ref, nq)
        carry_ref[0:1, cols] = _lru_scan(a, b, h_ref, functools.partial(_pair_cols, hp), p_ref.at[hp % 2],
                                         carry_ref[0:1, cols], nq, True)

    project(0)
    for hp in range(npairs):
        if hp + 1 < npairs:
            project(hp + 1)
        work(hp)


def _out_kernel(x_ref, xc_ref, hb_ref, m_ref, ng_ref, wzg_ref, wa_ref, ba_ref, wx_ref, bx_ref, cv_ref,
                wl_ref, wo_ref, gf_ref, perm_ref, o_ref, hf_ref, p_ref, carry_ref, v_ref, hn_ref, zl_ref,
                gl_ref):
    i = pl.program_id(1)
    nq = Q_OUT

    @pl.when(i == 0)
    def _():
        carry_ref[...] = jnp.zeros_like(carry_ref)

    _class_major_rows(x_ref, ng_ref, perm_ref, hn_ref, nq)

    npairs = LRU_HEADS // 2

    def project(hp):
        cols = slice(hp * LRU_PAIR, (hp + 1) * LRU_PAIR)
        gcols = slice(D_MODEL + hp * LRU_PAIR, D_MODEL + (hp + 1) * LRU_PAIR)
        zl_ref[:, cols] = jnp.dot(hn_ref[...], wzg_ref[:, cols], preferred_element_type=F32)
        gl_ref[:, cols] = jnp.dot(hn_ref[...], wzg_ref[:, gcols], preferred_element_type=F32)

    def work(hp):
        cols = slice(hp * LRU_PAIR, (hp + 1) * LRU_PAIR)
        a, b = _lru_gates(xc_ref, hp, wa_ref, ba_ref, wx_ref, bx_ref, cv_ref, nq)
        carry_ref[0:1, cols] = _lru_scan(a, b, hf_ref, functools.partial(_pair_cols, hp), p_ref.at[hp % 2],
                                         carry_ref[0:1, cols], nq, False)
        for r in range(SUB):
            hl = hf_ref[0, :, _pair_cols(hp, r)] + hb_ref[0, :, _pair_cols(hp, r)]
            v = hl * jax.nn.silu(zl_ref[r * nq:(r + 1) * nq, cols])
            v_ref[r * nq:(r + 1) * nq, cols] = v.astype(BF16)

    project(0)
    for hp in range(npairs):
        if hp + 1 < npairs:
            project(hp + 1)
        work(hp)

    ylru = jnp.dot(v_ref[...], wl_ref[...], preferred_element_type=F32)
    ms = []
    for r in range(SUB):
        sl = slice(r * D_MODEL, (r + 1) * D_MODEL)
        rows = slice(r * nq, (r + 1) * nq)
        ms.append((m_ref[0, :, sl] + jax.nn.sigmoid(gl_ref[rows, :]) * ylru[rows, :]).astype(BF16))
    for gi in range(nq // SUB):
        mg = jnp.concatenate([m[gi * SUB:(gi + 1) * SUB, :] for m in ms], axis=0)
        mn = _class_major(mg, perm_ref[...])
        dm = jnp.dot(mn, wo_ref[...], preferred_element_type=F32)
        xo = x_ref[0, gi * PERM:(gi + 1) * PERM, :] + dm
        o_ref[0, gi * PERM:(gi + 1) * PERM, :] = _rms(xo, gf_ref[...])


def _cx_mul(a, b):
    return a[0] * b[0] - a[1] * b[1], a[0] * b[1] + a[1] * b[0]


def _prep_s5(a_re, a_im, log_dt, b_re, b_im, c_re, c_im):
    hi = lax.Precision.HIGHEST
    g = S5_GROUPS
    are, aim = a_re.astype(F32), a_im.astype(F32)
    dt = jnp.exp(log_dt.astype(F32))[..., None]
    lre, lim = are * dt, aim * dt

    def apow(n):
        nn = n.astype(F32)[None, None, :, None]
        mag = jnp.exp(lre[:, :, None, :] * nn)
        ang = lim[:, :, None, :] * nn
        return mag * jnp.cos(ang), mag * jnp.sin(ang)

    ab = (jnp.exp(lre) * jnp.cos(lim), jnp.exp(lre) * jnp.sin(lim))
    den = are * are + aim * aim
    quo = (((ab[0] - 1.0) * are + ab[1] * aim) / den, (ab[1] * are - (ab[0] - 1.0) * aim) / den)
    bbar = _cx_mul((quo[0][..., None], quo[1][..., None]), (b_re.astype(F32), b_im.astype(F32)))
    cmat = (c_re.astype(F32), c_im.astype(F32))

    pw = apow(jnp.arange(SUB + 1))
    cp = _cx_mul((cmat[0][:, :, None], cmat[1][:, :, None]),
                 (pw[0][:, :, :SUB, None, :], pw[1][:, :, :SUB, None, :]))
    kern = (jnp.einsum('dgkip,dgpj->dgkij', cp[0], bbar[0], precision=hi)
            - jnp.einsum('dgkip,dgpj->dgkij', cp[1], bbar[1], precision=hi))
    kf, kb = kern[0], kern[1]
    by_lag = jnp.concatenate([kf[:, :0:-1], kf[:, :1] + kb[:, :1], kb[:, 1:]], axis=1)
    m = jnp.stack([by_lag[:, SUB - 1 - rp:2 * SUB - 1 - rp] for rp in range(SUB)], axis=1)
    mt = m.transpose(0, 1, 3, 2, 4).reshape(g, 256, 256)

    def cat(zf, zb, sign=1.0):
        return jnp.concatenate([zf[0], zb[0], sign * zf[1], sign * zb[1]], axis=-1)

    def sel(z, d, fn):
        return fn(z[0][d]), fn(z[1][d])

    bt = [(bbar[0][d].transpose(0, 2, 1)[:, None], bbar[1][d].transpose(0, 2, 1)[:, None])
          for d in range(2)]
    wf = _cx_mul(sel(pw, 0, lambda z: z[:, ::-1][:, 1:, None, :]), bt[0])
    wb = _cx_mul(sel(pw, 1, lambda z: z[:, :SUB, None, :]), bt[1])
    wbt = cat(wf, wb).reshape(g, 256, 256).transpose(0, 2, 1)
    cf = _cx_mul(sel(cmat, 0, lambda z: z[:, None]), sel(pw, 0, lambda z: z[:, 1:, None, :]))
    cb = _cx_mul(sel(cmat, 1, lambda z: z[:, None]),
                 sel(pw, 1, lambda z: z[:, ::-1][:, :SUB, None, :]))
    planes = jnp.stack([cf[0], -cf[1], cb[0], -cb[1]], axis=3)
    planes = planes.reshape(g // 2, 2, 256, 4, S5_STATE)
    wct = jnp.einsum('aerkp,ef->aerkfp', planes, jnp.eye(2, dtype=F32)).reshape(g // 2, 512, 512)

    def pair_lanes(z):
        rows = z.shape[1]
        return z.reshape(g // 2, 2, rows, S5_STATE).transpose(0, 2, 1, 3).reshape(g // 2, rows, 2 * S5_STATE)

    def tab(nf, nb, mf, mb, backward_only=False):
        zf = sel(apow(nf), 0, lambda z: pair_lanes(z * mf[None, :, None]))
        zb = sel(apow(nb), 1, lambda z: pair_lanes(z * mb[None, :, None]))
        return jnp.concatenate(([] if backward_only else [zf[0], zf[1]]) + [zb[0], zb[1]], axis=-1)

    m8 = jnp.arange(SUBLANES)
    nblk = Q_S5 // SUBLANES
    m16 = jnp.arange(nblk)
    parts = []
    for s in (1, 2, 4):
        n = jnp.full((SUBLANES,), SUB * s)
        parts.append(tab(n, n, (m8 >= s).astype(F32), (m8 < SUBLANES - s).astype(F32)))
    for s in (1, 2, 4, 8):
        n = jnp.full((nblk,), SUB * SUBLANES * s)
        parts.append(tab(n, n, (m16 >= s).astype(F32), (m16 < nblk - s).astype(F32)))
    one8 = jnp.ones((SUBLANES,), F32)
    parts.append(tab(SUB * (m8 + 1), SUB * (SUBLANES - m8), one8, one8))
    for n in (SUB, SUB * Q_S5):
        nn = jnp.full((SUBLANES,), n)
        parts.append(tab(nn, nn, one8, one8))
    table = jnp.concatenate(parts, axis=1)
    nq = SUB * jnp.arange(Q_S5)
    wq = tab(nq, nq, jnp.ones((Q_S5,), F32), jnp.ones((Q_S5,), F32), backward_only=True)
    return mt.astype(BF16), wbt.astype(BF16), wct.astype(BF16), table, wq


def _pair_heads(w):
    w = w.astype(BF16).reshape(2, LRU_HEADS // 2, 2, LRU_BLOCK, LRU_BLOCK)
    z = jnp.zeros_like(w[:, :, 0])
    top = jnp.concatenate([w[:, :, 0], z], axis=-1)
    bot = jnp.concatenate([z, w[:, :, 1]], axis=-1)
    return jnp.concatenate([top, bot], axis=-2)


def _full(shape):
    return pl.BlockSpec(shape, lambda *_: (0,) * len(shape))


def _params(sem, **kw):
    return pltpu.CompilerParams(dimension_semantics=sem, vmem_limit_bytes=VMEM_LIMIT, **kw)


def _trunk(x, w):
    bsz, seq, _ = x.shape
    lq = seq // SUB
    assert seq % (SUB * Q_S5) == 0
    row_shape = jax.ShapeDtypeStruct((bsz, lq, SUB * D_MODEL), F32)

    nt = lq // Q_S5
    uz = pl.pallas_call(
        _in_t_kernel,
        grid=(bsz, nt, 2),
        in_specs=[pl.BlockSpec((1, SUB * Q_S5, D_MODEL), lambda b, t, c: (b, t, 0)),
                  _full((1, D_MODEL)), _full((PERM, PERM)),
                  pl.BlockSpec((D_MODEL, D_MODEL), lambda b, t, c: (c, 0))],
        out_specs=pl.BlockSpec((1, SUB, D_MODEL, Q_S5), lambda b, t, c: (b, 0, c, t)),
        out_shape=jax.ShapeDtypeStruct((bsz, SUB, 2 * D_MODEL, lq), F32),
        scratch_shapes=[pltpu.VMEM((SUB * Q_S5, D_MODEL), BF16)],
        compiler_params=_params(("parallel", "parallel", "arbitrary")),
        name="in_proj_t",
    )(x, w["norm_g"], w["perm"], w["w_s5_t"])

    rows = pl.BlockSpec((1, Q_IN, SUB * D_MODEL), lambda b, t: (b, t, 0))
    g_s5 = pl.pallas_call(
        _in_n_kernel,
        grid=(bsz, lq // Q_IN),
        in_specs=[pl.BlockSpec((1, SUB * Q_IN, D_MODEL), lambda b, t: (b, t, 0)),
                  _full((1, D_MODEL)), _full((PERM, PERM)), _full((D_MODEL, D_MODEL))],
        out_specs=rows,
        out_shape=row_shape,
        compiler_params=_params(("parallel", "parallel")),
        name="in_proj_n",
    )(x, w["norm_g"], w["perm"], w["w_gs5"])

    def tile_of(ph, i):
        return jnp.where(ph == 0, nt - 1 - i, i)
    npair = GROUPS_PER_STEP // 2
    slab = GROUPS_PER_STEP * S5_GROUP_SIZE
    gw = pl.BlockSpec((GROUPS_PER_STEP, 256, 256), lambda g, b, ph, i: (g, 0, 0))
    y_t = pl.pallas_call(
        functools.partial(_s5_kernel, nt=nt),
        grid=(S5_GROUPS // GROUPS_PER_STEP, bsz, 2, nt),
        in_specs=[pl.BlockSpec((1, SUB, slab, Q_S5), lambda g, b, ph, i: (b, 0, g, tile_of(ph, i))),
                  pl.BlockSpec((slab, LANES), lambda g, b, ph, i: (g, 0)),
                  gw, gw,
                  pl.BlockSpec((npair, 512, 512), lambda g, b, ph, i: (g, 0, 0)),
                  pl.BlockSpec((npair, T_ROWS, 4 * LANES), lambda g, b, ph, i: (g, 0, 0)),
                  pl.BlockSpec((npair, Q_S5, 2 * LANES), lambda g, b, ph, i: (g, 0, 0))],
        out_specs=pl.BlockSpec((1, SUB, slab, Q_S5), lambda g, b, ph, i: (b, 0, g, ph * i)),
        out_shape=jax.ShapeDtypeStruct((bsz, SUB, D_MODEL, lq), F32),
        scratch_shapes=[pltpu.VMEM((npair, Q_S5, 4 * LANES), F32),
                        pltpu.VMEM((npair, SUBLANES, 2 * LANES), F32),
                        pltpu.VMEM((npair, SUBLANES, 2 * LANES), F32),
                        pltpu.VMEM((npair, SUBLANES, 2 * LANES), F32),
                        pltpu.VMEM((nt, npair, SUBLANES, 2 * LANES), F32)],
        compiler_params=_params(("arbitrary",) * 4),
        name="s5_ssm",
    )(uz, w["s5_d_b"], w["s5_mt"], w["s5_wbt"], w["s5_wct"], w["s5_tab"], w["s5_wq"])

    m_s5 = pl.pallas_call(
        _s5_post_kernel,
        grid=(bsz, nt, SUB // R_BLK),
        in_specs=[pl.BlockSpec((1, R_BLK, D_MODEL, Q_S5), lambda b, t, r: (b, r, 0, t)),
                  pl.BlockSpec((1, R_BLK, D_MODEL, Q_S5), lambda b, t, r: (b, r, 1, t)),
                  pl.BlockSpec((1, Q_S5, R_BLK * D_MODEL), lambda b, t, r: (b, t, r)),
                  _full((D_MODEL, D_MODEL)), _full((D_MODEL, LANES)), _full((D_MODEL, D_MODEL))],
        out_specs=pl.BlockSpec((1, Q_S5, R_BLK * D_MODEL), lambda b, t, r: (b, t, r)),
        out_shape=row_shape,
        compiler_params=_params(("parallel", "parallel", "parallel")),
        name="s5_post",
    )(y_t, uz, g_s5, w["glu_wt"], w["glu_b_b"], w["s5_proj"])

    ntb = lq // Q_LRU_B
    rows_b = pl.BlockSpec((1, Q_LRU_B, SUB * D_MODEL), lambda b, i: (b, ntb - 1 - i, 0))
    gate_w = _full((LRU_HEADS // 2, LRU_PAIR, LRU_PAIR))
    gate_b = _full((LRU_HEADS // 2, 1, LRU_PAIR))
    xc, h_bwd = pl.pallas_call(
        functools.partial(_lru_bwd_kernel, nt=ntb),
        grid=(bsz, ntb),
        in_specs=[pl.BlockSpec((1, SUB * Q_LRU_B, D_MODEL), lambda b, i: (b, ntb - 1 - i, 0)),
                  pl.BlockSpec((1, SUB, D_MODEL),
                               lambda b, i: (b, jnp.maximum((ntb - 1 - i) * Q_LRU_B - 1, 0), 0)),
                  _full((1, D_MODEL)), _full((PERM, PERM)), _full((D_MODEL, D_MODEL)),
                  _full((4, D_MODEL)), _full((1, D_MODEL)),
                  gate_w, gate_b, gate_w, gate_b, gate_b],
        out_specs=[rows_b, rows_b],
        out_shape=[row_shape, row_shape],
        scratch_shapes=[pltpu.VMEM((1, Q_LRU_B, SUB * D_MODEL), F32),
                        pltpu.VMEM((LRU_HEADS // 2, SUB, LRU_PAIR), F32),
                        pltpu.VMEM((SUB * Q_LRU_B, D_MODEL), BF16),
                        pltpu.VMEM((2, SUB, Q_LRU_B, LRU_PAIR), F32),
                        pltpu.VMEM((SUBLANES, D_MODEL), F32),
                        pltpu.VMEM((SUBLANES, D_MODEL), F32)],
        compiler_params=_params(("arbitrary", "arbitrary")),
        name="lru_bwd",
    )(x, x, w["norm_g"], w["perm"], w["w_ulru"], w["conv_w"], w["conv_b"],
      w["wa"][1], w["ba"][1], w["wx"][1], w["bx"][1], w["cvec"][1])

    rows_o = pl.BlockSpec((1, Q_OUT, SUB * D_MODEL), lambda b, i: (b, i, 0))
    x_o = pl.BlockSpec((1, SUB * Q_OUT, D_MODEL), lambda b, i: (b, i, 0))
    out = pl.pallas_call(
        _out_kernel,
        grid=(bsz, lq // Q_OUT),
        in_specs=[x_o] + [rows_o] * 3 + [_full((1, D_MODEL)), _full((D_MODEL, 2 * D_MODEL)),
                                         gate_w, gate_b, gate_w, gate_b, gate_b,
                                         _full((D_MODEL, D_MODEL)), _full((D_MODEL, D_MODEL)),
                                         _full((1, D_MODEL)), _full((PERM, PERM))],
        out_specs=x_o,
        out_shape=jax.ShapeDtypeStruct((bsz, seq, D_MODEL), F32),
        scratch_shapes=[pltpu.VMEM((1, Q_OUT, SUB * D_MODEL), F32),
                        pltpu.VMEM((2, SUB, Q_OUT, LRU_PAIR), F32),
                        pltpu.VMEM((SUBLANES, D_MODEL), F32),
                        pltpu.VMEM((SUB * Q_OUT, D_MODEL), BF16),
                        pltpu.VMEM((SUB * Q_OUT, D_MODEL), BF16),
                        pltpu.VMEM((SUB * Q_OUT, D_MODEL), F32),
                        pltpu.VMEM((SUB * Q_OUT, D_MODEL), F32)],
        compiler_params=_params(("arbitrary", "arbitrary")),
        name="lru_fwd_out",
    )(x, xc, h_bwd, m_s5, w["norm_g"], w["w_zg"], w["wa"][0], w["ba"][0], w["wx"][0], w["bx"][0],
      w["cvec"][0], w["lru_proj"], w["w_out"], w["norm_f_g"], w["perm"])
    return out


def kernel(x_prompt, x_sample, norm_g, w_in, s5_a_re, s5_a_im, s5_log_dt, s5_b_re, s5_b_im, s5_c_re, s5_c_im, s5_d, s5_glu_w, s5_glu_b, s5_proj, lru_conv_w, lru_conv_b, lru_lambda, lru_wa, lru_ba, lru_wx, lru_bx, lru_proj, w_out, norm_f_g):
    assert norm_g.shape[0] == 1, "single-layer trunk"
    wi = w_in[0]
    mt, wbt, wct, tab, wq = _prep_s5(s5_a_re[0], s5_a_im[0], s5_log_dt[0], s5_b_re[0], s5_b_im[0],
                                     s5_c_re[0], s5_c_im[0])
    tok = jnp.arange(PERM)
    w = {
        "perm": (tok[None, :] == SUB * (tok % SUB)[:, None] + (tok // SUB)[:, None]).astype(BF16),
        "norm_g": norm_g[0].reshape(1, D_MODEL).astype(F32),
        "w_s5_t": wi[:, :2 * D_MODEL].T.astype(BF16),
        "w_ulru": wi[:, 2 * D_MODEL:3 * D_MODEL].astype(BF16),
        "w_gs5": wi[:, 4 * D_MODEL:5 * D_MODEL].astype(BF16),
        "w_zg": jnp.concatenate([wi[:, 3 * D_MODEL:4 * D_MODEL], wi[:, 5 * D_MODEL:]], axis=1).astype(BF16),
        "s5_mt": mt, "s5_wbt": wbt, "s5_wct": wct, "s5_tab": tab, "s5_wq": wq,
        "s5_d_b": jnp.broadcast_to(s5_d[0].astype(F32)[:, None], (D_MODEL, LANES)),
        "glu_wt": s5_glu_w[0].T.astype(BF16),
        "glu_b_b": jnp.broadcast_to(s5_glu_b[0].astype(F32)[:, None], (D_MODEL, LANES)),
        "s5_proj": s5_proj[0].astype(BF16),
        "conv_w": lru_conv_w[0].astype(F32),
        "conv_b": lru_conv_b[0].reshape(1, D_MODEL).astype(F32),
        "wa": _pair_heads(lru_wa[0]),
        "wx": _pair_heads(lru_wx[0]),
        "ba": lru_ba[0].reshape(2, LRU_HEADS // 2, 1, LRU_PAIR).astype(F32),
        "bx": lru_bx[0].reshape(2, LRU_HEADS // 2, 1, LRU_PAIR).astype(F32),
        "cvec": (-RG_C * jax.nn.softplus(-lru_lambda[0].astype(F32))).reshape(2, LRU_HEADS // 2, 1, LRU_PAIR),
        "lru_proj": lru_proj[0].astype(BF16),
        "w_out": w_out[0].astype(BF16),
        "norm_f_g": norm_f_g.reshape(1, D_MODEL).astype(F32),
    }
    return (_trunk(x_prompt, w), _trunk(x_sample, w))
```

```python
import functools

import jax
import jax.numpy as jnp
from jax import lax
from jax.experimental import pallas as pl
from jax.experimental.pallas import tpu as pltpu

F32 = jnp.float32
BF16 = jnp.bfloat16

D_MODEL = 1024
SUB = 16
PERM = SUB * SUB
S5_GROUPS = 64
S5_GROUP_SIZE = 16
S5_STATE = 64
GROUPS_PER_STEP = 16
LRU_HEADS = 8
LRU_BLOCK = 128
LRU_PAIR = 2 * LRU_BLOCK
RG_C = 8.0
EPS = 1e-6
LANES = 128
SUBLANES = 8
Q_S5 = 128
R_BLK = 4
Q_IN = 64
Q_LRU_B = 64
Q_OUT = 32
VMEM_LIMIT = 56 * 1024 * 1024

T_STEP = 0
T_BLK = 24
T_FIX = 88
T_A16 = 96
T_AQ = 104
T_ROWS = 112


def _rms(x, g):
    return x * lax.rsqrt(jnp.mean(x * x, axis=-1, keepdims=True) + EPS) * g


def _class_major(h, perm):
    return jnp.dot(perm, h, preferred_element_type=F32).astype(BF16)


def _in_t_kernel(x_ref, g_ref, perm_ref, w_ref, o_ref, h_ref):
    @pl.when(pl.program_id(2) == 0)
    def _():
        _class_major_rows(x_ref, g_ref, perm_ref, h_ref, Q_S5)

    rows = R_BLK * Q_S5
    for rb in range(SUB // R_BLK):
        res = lax.dot_general(w_ref[...], h_ref[rb * rows:(rb + 1) * rows, :],
                              (((1,), (1,)), ((), ())), preferred_element_type=F32)
        for rr in range(R_BLK):
            o_ref[0, rb * R_BLK + rr] = res[:, rr * Q_S5:(rr + 1) * Q_S5]


def _in_n_kernel(x_ref, g_ref, perm_ref, w_ref, o_ref):
    for gi in range(Q_IN // SUB):
        xg = x_ref[0, gi * PERM:(gi + 1) * PERM, :]
        h = _class_major(_rms(xg, g_ref[...]).astype(BF16), perm_ref[...])
        res = jnp.dot(h, w_ref[...], preferred_element_type=F32)
        for r in range(SUB):
            o_ref[0, gi * SUB:(gi + 1) * SUB, r * D_MODEL:(r + 1) * D_MODEL] = res[r * SUB:(r + 1) * SUB, :]


def _s5_kernel(u_ref, d_ref, mt_ref, wbt_ref, wct_ref, tab_ref, wq_ref, o_ref,
               s_ref, cf_ref, cn_ref, cb_ref, hb_ref, *, nt):
    ph = pl.program_id(2)
    i = pl.program_id(3)
    tile = jnp.where(ph == 0, nt - 1 - i, i)
    q = Q_S5
    nblk = q // SUBLANES

    @pl.when(i == 0)
    def _():
        cf_ref[...] = jnp.zeros_like(cf_ref)
        cb_ref[...] = jnp.zeros_like(cb_ref)

    npair = GROUPS_PER_STEP // 2
    st = S5_STATE
    f_re, f_im, b_re, b_im = (slice(k * LANES, (k + 1) * LANES) for k in range(4))
    re, im = slice(0, LANES), slice(LANES, 2 * LANES)

    def cmul(cre, cim, xre, xim):
        return cre * xre - cim * xim, cre * xim + cim * xre

    def load_x(gl):
        c0 = gl * S5_GROUP_SIZE
        xt = u_ref[0, :, c0:c0 + S5_GROUP_SIZE, :]
        return c0, xt, xt.reshape(SUB * S5_GROUP_SIZE, q).astype(BF16)

    def plane(parts, k):
        return jnp.concatenate([p[k * st:(k + 1) * st, :] for p in parts], axis=0).T

    @pl.when(ph == 0)
    def _():
        for m in range(npair):
            cb = cb_ref[m]
            parts = []
            for gl in (2 * m, 2 * m + 1):
                wb = jnp.concatenate([wbt_ref[gl, st:2 * st, :], wbt_ref[gl, 3 * st:4 * st, :]], axis=0)
                parts.append(jnp.dot(wb, load_x(gl)[2], preferred_element_type=F32))
            hb_ref[tile, m] = cb
            w = wq_ref[m]
            pre, pim = cmul(w[:, re], w[:, im], plane(parts, 0), plane(parts, 1))
            red = jnp.concatenate([jnp.sum(pre, axis=0, keepdims=True),
                                   jnp.sum(pim, axis=0, keepdims=True)], axis=1)
            aq = tab_ref[m, T_AQ:T_AQ + SUBLANES, :]
            cre, cim = cmul(aq[:, b_re], aq[:, b_im], cb[:, re], cb[:, im])
            cb_ref[m] = (jnp.broadcast_to(red, (SUBLANES, 2 * LANES))
                         + jnp.concatenate([cre, cim], axis=1))

    @pl.when(ph == 1)
    def _():
        row8 = lax.broadcasted_iota(jnp.int32, (SUBLANES, LANES), 0)
        row16 = lax.broadcasted_iota(jnp.int32, (nblk, 2 * LANES), 0)
        rowq = lax.broadcasted_iota(jnp.int32, (q, 2 * LANES), 0)


        def stage_states(m):
            parts = []
            for gl in (2 * m, 2 * m + 1):
                c0, xt, xb = load_x(gl)
                parts.append(jnp.dot(wbt_ref[gl], xb, preferred_element_type=F32))
                y = jnp.dot(mt_ref[gl], xb, preferred_element_type=F32)
                dv = d_ref[c0:c0 + S5_GROUP_SIZE, :]
                o_ref[0, :, c0:c0 + S5_GROUP_SIZE, :] = (
                    y.reshape(SUB, S5_GROUP_SIZE, q) + xt * dv[None])
            for k, dst in enumerate((f_re, b_re, f_im, b_im)):
                s_ref[m, :, dst] = plane(parts, k)

        def stage_scan(m):
            sc = s_ref.at[m]
            cf_in, cb_in = cf_ref[m], hb_ref[tile, m]
            a16 = tab_ref[m, T_A16:T_A16 + SUBLANES, :]
            tre, tim = cmul(a16[:, f_re], a16[:, f_im], cf_in[:, re], cf_in[:, im])
            sc[0:SUBLANES, f_re] += jnp.where(row8 == 0, tre, 0.0)
            sc[0:SUBLANES, f_im] += jnp.where(row8 == 0, tim, 0.0)
            tre, tim = cmul(a16[:, b_re], a16[:, b_im], cb_in[:, re], cb_in[:, im])
            sc[q - SUBLANES:q, b_re] += jnp.where(row8 == SUBLANES - 1, tre, 0.0)
            sc[q - SUBLANES:q, b_im] += jnp.where(row8 == SUBLANES - 1, tim, 0.0)

            steps = [tab_ref[m, T_STEP + 8 * si:T_STEP + 8 * si + 8, :] for si in range(3)]
            ends_f, ends_b = [], []
            for k in range(nblk):
                rows = slice(k * SUBLANES, (k + 1) * SUBLANES)
                fr, fi, br, bi = sc[rows, f_re], sc[rows, f_im], sc[rows, b_re], sc[rows, b_im]
                for si, s in enumerate((1, 2, 4)):
                    c = steps[si]
                    dr, di = cmul(c[:, f_re], c[:, f_im], pltpu.roll(fr, s, 0), pltpu.roll(fi, s, 0))
                    fr, fi = fr + dr, fi + di
                    dr, di = cmul(c[:, b_re], c[:, b_im],
                                  pltpu.roll(br, SUBLANES - s, 0), pltpu.roll(bi, SUBLANES - s, 0))
                    br, bi = br + dr, bi + di
                sc[rows, :] = jnp.concatenate([fr, fi, br, bi], axis=1)
                ends_f.append(jnp.concatenate([fr[SUBLANES - 1:], fi[SUBLANES - 1:]], axis=1))
                ends_b.append(jnp.concatenate([br[0:1], bi[0:1]], axis=1))

            ef = jnp.concatenate(ends_f, axis=0)
            eb = jnp.concatenate(ends_b, axis=0)
            for si, s in enumerate((1, 2, 4, 8)):
                c = tab_ref[m, T_BLK + 16 * si:T_BLK + 16 * si + 16, :]
                dr, di = cmul(c[:, f_re], c[:, f_im], pltpu.roll(ef[:, re], s, 0), pltpu.roll(ef[:, im], s, 0))
                ef = ef + jnp.concatenate([dr, di], axis=1)
                dr, di = cmul(c[:, b_re], c[:, b_im],
                              pltpu.roll(eb[:, re], nblk - s, 0), pltpu.roll(eb[:, im], nblk - s, 0))
                eb = eb + jnp.concatenate([dr, di], axis=1)
            pf = jnp.where(row16 >= 1, pltpu.roll(ef, 1, 0), 0.0)
            pb = jnp.where(row16 <= nblk - 2, pltpu.roll(eb, nblk - 1, 0), 0.0)
            fix = tab_ref[m, T_FIX:T_FIX + SUBLANES, :]
            for k in range(nblk):
                rows = slice(k * SUBLANES, (k + 1) * SUBLANES)
                vf = jnp.broadcast_to(pf[k:k + 1, :], (SUBLANES, 2 * LANES))
                vb = jnp.broadcast_to(pb[k:k + 1, :], (SUBLANES, 2 * LANES))
                fr, fi = cmul(fix[:, f_re], fix[:, f_im], vf[:, re], vf[:, im])
                br, bi = cmul(fix[:, b_re], fix[:, b_im], vb[:, re], vb[:, im])
                sc[rows, :] += jnp.concatenate([fr, fi, br, bi], axis=1)
            cn_ref[m] = jnp.broadcast_to(ef[nblk - 1:nblk, :], (SUBLANES, 2 * LANES))

        def stage_out(m):
            h = s_ref[m]
            dn = jnp.where(rowq == 0, jnp.broadcast_to(cf_ref[m, 0:1, :], (q, 2 * LANES)),
                           pltpu.roll(h[:, :2 * LANES], 1, 0))
            up = jnp.where(rowq == q - 1, jnp.broadcast_to(hb_ref[tile, m, 0:1, :], (q, 2 * LANES)),
                           pltpu.roll(h[:, 2 * LANES:], q - 1, 0))
            hprev = jnp.concatenate([dn, up], axis=1).astype(BF16)
            y = lax.dot_general(wct_ref[m], hprev, (((1,), (1,)), ((), ())),
                                preferred_element_type=F32)
            for e in range(2):
                c0 = (2 * m + e) * S5_GROUP_SIZE
                o_ref[0, :, c0:c0 + S5_GROUP_SIZE, :] += (
                    y[e * 256:(e + 1) * 256, :].reshape(SUB, S5_GROUP_SIZE, q))

        for step in range(npair + 2):
            if step < npair:
                stage_states(step)
            if 0 <= step - 1 < npair:
                stage_scan(step - 1)
            if 0 <= step - 2 < npair:
                stage_out(step - 2)
        cf_ref[...] = cn_ref[...]


def _s5_post_kernel(y_ref, z_ref, g_ref, wg_ref, bg_ref, wp_ref, o_ref):
    ys = [jax.nn.gelu(y_ref[0, rr]) for rr in range(R_BLK)]
    y = jnp.concatenate(ys, axis=1)
    glu = jnp.dot(wg_ref[...], y.astype(BF16), preferred_element_type=F32)
    bias = jnp.concatenate([bg_ref[...]] * R_BLK, axis=1)
    y = y * jax.nn.sigmoid(glu + bias)
    z = jnp.concatenate([z_ref[0, rr] for rr in range(R_BLK)], axis=1)
    y = (y * jax.nn.silu(z)).astype(BF16)
    ys5 = lax.dot_general(y, wp_ref[...], (((0,), (0,)), ((), ())),
                          preferred_element_type=F32)
    for rr in range(R_BLK):
        g = g_ref[0, :, rr * D_MODEL:(rr + 1) * D_MODEL]
        o_ref[0, :, rr * D_MODEL:(rr + 1) * D_MODEL] = (
            jax.nn.sigmoid(g) * ys5[rr * Q_S5:(rr + 1) * Q_S5, :])


def _pair_cols(hp, r):
    return slice(r * D_MODEL + hp * LRU_PAIR, r * D_MODEL + (hp + 1) * LRU_PAIR)


def _lru_gates(xc_ref, hp, wa_ref, ba_ref, wx_ref, bx_ref, cv_ref, nq):
    xc = jnp.concatenate([xc_ref[0, :, _pair_cols(hp, r)] for r in range(SUB)], axis=0)
    xb = xc.astype(BF16)

    ra = jnp.dot(xb, wa_ref[hp], preferred_element_type=F32) + ba_ref[hp]
    rx = jnp.dot(xb, wx_ref[hp], preferred_element_type=F32) + bx_ref[hp]
    log_a = cv_ref[hp] * jax.nn.sigmoid(ra)
    a = jnp.exp(log_a)
    gap = 1.0 - a * a
    root = jnp.where(gap > 0.0, gap * lax.rsqrt(gap), 0.0)
    b = root * (jax.nn.sigmoid(rx) * xc)
    return a, b


def _lru_scan(a, b, h_ref, col_of, p_ref, carry_row, nq, reverse):
    order = list(range(SUB - 1, -1, -1)) if reverse else list(range(SUB))
    h = None
    p = None
    for r in order:
        ar = a[r * nq:(r + 1) * nq, :]
        br = b[r * nq:(r + 1) * nq, :]
        h = br if h is None else ar * h + br
        p = ar if p is None else ar * p
        h_ref[0, :, col_of(r)] = h
        p_ref[r] = p

    row = lax.broadcasted_iota(jnp.int32, (nq, LRU_PAIR), 0)
    s = 1
    while s < nq:
        keep = (row < nq - s) if reverse else (row >= s)
        shift = (nq - s) if reverse else s
        hs = jnp.where(keep, pltpu.roll(h, shift, 0), 0.0)
        ps = jnp.where(keep, pltpu.roll(p, shift, 0), 1.0)
        h = h + p * hs
        p = p * ps
        s *= 2
    e = h + p * carry_row
    if reverse:
        cin = jnp.where(row < nq - 1, pltpu.roll(e, nq - 1, 0), carry_row)
        carry_out = e[0:1, :]
    else:
        cin = jnp.where(row >= 1, pltpu.roll(e, 1, 0), carry_row)
        carry_out = e[nq - 1:nq, :]
    for r in order:
        h_ref[0, :, col_of(r)] = h_ref[0, :, col_of(r)] + p_ref[r] * cin
    return carry_out


def _class_major_rows(x_ref, g_ref, perm_ref, h_ref, nq, on_group=None):
    for gi in range(nq // SUB):
        xg = x_ref[0, gi * PERM:(gi + 1) * PERM, :]
        hp = _class_major(_rms(xg, g_ref[...]).astype(BF16), perm_ref[...])
        if on_group is not None:
            on_group(gi, hp)
        for r in range(SUB):
            h_ref[r * nq + gi * SUB:r * nq + (gi + 1) * SUB, :] = hp[r * SUB:(r + 1) * SUB, :]


def _lru_bwd_kernel(x_ref, xh_ref, g_ref, perm_ref, wu_ref, cw_ref, cbias_ref,
                    wa_ref, ba_ref, wx_ref, bx_ref, cv_ref,
                    xc_ref, h_ref, u_ref, uh_ref, hn_ref, p_ref, nx_ref, carry_ref, *, nt):
    i = pl.program_id(1)
    tile = nt - 1 - i
    nq = Q_LRU_B

    @pl.when(i == 0)
    def _():
        nx_ref[...] = jnp.zeros_like(nx_ref)
        carry_ref[...] = jnp.zeros_like(carry_ref)

    def project_group0(gi, hp):
        res = jnp.dot(hp, wu_ref[:, 0:LRU_PAIR], preferred_element_type=F32)
        for r in range(SUB):
            u_ref[0, gi * SUB:(gi + 1) * SUB, _pair_cols(0, r)] = res[r * SUB:(r + 1) * SUB, :]

    _class_major_rows(x_ref, g_ref, perm_ref, hn_ref, nq, on_group=project_group0)
    hh = _rms(xh_ref[0], g_ref[...]).astype(BF16)
    live = (tile > 0).astype(F32)
    row = lax.broadcasted_iota(jnp.int32, (nq, LRU_PAIR), 0)
    npairs = LRU_HEADS // 2

    def project_halo(hp):
        cols = slice(hp * LRU_PAIR, (hp + 1) * LRU_PAIR)
        uh_ref[hp] = jnp.dot(hh, wu_ref[:, cols], preferred_element_type=F32) * live

    def project(hp):
        cols = slice(hp * LRU_PAIR, (hp + 1) * LRU_PAIR)
        for rb in range(SUB // R_BLK):
            res = jnp.dot(hn_ref[rb * R_BLK * nq:(rb + 1) * R_BLK * nq, :], wu_ref[:, cols],
                          preferred_element_type=F32)
            for rr in range(R_BLK):
                u_ref[0, :, _pair_cols(hp, rb * R_BLK + rr)] = res[rr * nq:(rr + 1) * nq, :]
        project_halo(hp)

    def work(hp):
        cols = slice(hp * LRU_PAIR, (hp + 1) * LRU_PAIR)

        def cls(r):
            return u_ref[0, :, _pair_cols(hp, r)]
        uh = uh_ref[hp]
        taps = {-2: jnp.where(row == 0, uh[14:15, :], pltpu.roll(cls(14), 1, 0)),
                -1: jnp.where(row == 0, uh[15:16, :], pltpu.roll(cls(15), 1, 0)),
                SUB: jnp.where(row == nq - 1, nx_ref[0:1, cols], pltpu.roll(cls(0), nq - 1, 0))}
        nx_ref[0:1, cols] = u_ref[0, 0:1, _pair_cols(hp, 0)]

        def tap(r):
            return taps[r] if r in taps else cls(r)
        w = cw_ref[:, cols]
        for r in range(SUB):
            xc_ref[0, :, _pair_cols(hp, r)] = (
                cbias_ref[:, cols] + tap(r - 2) * w[0:1] + tap(r - 1) * w[1:2]
                + tap(r) * w[2:3] + tap(r + 1) * w[3:4])
        a, b = _lru_gates(xc_ref, hp, wa_ref, ba_ref, wx_ref, bx_ref, cv_ref, nq)
        carry_ref[0:1, cols] = _lru_scan(a, b, h_ref, functools.partial(_pair_cols, hp), p_ref.at[hp % 2],
                                         carry_ref[0:1, cols], nq, True)

    project_halo(0)
    for hp in range(npairs):
        if hp + 1 < npairs:
            project(hp + 1)
        work(hp)


def _out_kernel(x_ref, xc_ref, hb_ref, m_ref, ng_ref, wzg_ref, wa_ref, ba_ref, wx_ref, bx_ref, cv_ref,
                wl_ref, wo_ref, gf_ref, perm_ref, o_ref, hf_ref, p_ref, carry_ref, v_ref, hn_ref, zl_ref,
                gl_ref):
    i = pl.program_id(1)
    nq = Q_OUT

    @pl.when(i == 0)
    def _():
        carry_ref[...] = jnp.zeros_like(carry_ref)

    def project_group0(gi, hp):
        for dst, c in ((zl_ref, 0), (gl_ref, D_MODEL)):
            res = jnp.dot(hp, wzg_ref[:, c:c + LRU_PAIR], preferred_element_type=F32)
            for r in range(SUB):
                dst[r * nq + gi * SUB:r * nq + (gi + 1) * SUB, 0:LRU_PAIR] = res[r * SUB:(r + 1) * SUB, :]

    _class_major_rows(x_ref, ng_ref, perm_ref, hn_ref, nq, on_group=project_group0)

    npairs = LRU_HEADS // 2

    def project(hp):
        cols = slice(hp * LRU_PAIR, (hp + 1) * LRU_PAIR)
        gcols = slice(D_MODEL + hp * LRU_PAIR, D_MODEL + (hp + 1) * LRU_PAIR)
        zl_ref[:, cols] = jnp.dot(hn_ref[...], wzg_ref[:, cols], preferred_element_type=F32)
        gl_ref[:, cols] = jnp.dot(hn_ref[...], wzg_ref[:, gcols], preferred_element_type=F32)

    def work(hp):
        cols = slice(hp * LRU_PAIR, (hp + 1) * LRU_PAIR)
        a, b = _lru_gates(xc_ref, hp, wa_ref, ba_ref, wx_ref, bx_ref, cv_ref, nq)
        carry_ref[0:1, cols] = _lru_scan(a, b, hf_ref, functools.partial(_pair_cols, hp), p_ref.at[hp % 2],
                                         carry_ref[0:1, cols], nq, False)
        for r in range(SUB):
            hl = hf_ref[0, :, _pair_cols(hp, r)] + hb_ref[0, :, _pair_cols(hp, r)]
            v = hl * jax.nn.silu(zl_ref[r * nq:(r + 1) * nq, cols])
            v_ref[r * nq:(r + 1) * nq, cols] = v.astype(BF16)

    for hp in range(npairs):
        if hp + 1 < npairs:
            project(hp + 1)
        work(hp)

    ylru = jnp.dot(v_ref[...], wl_ref[...], preferred_element_type=F32)
    ms = []
    for r in range(SUB):
        sl = slice(r * D_MODEL, (r + 1) * D_MODEL)
        rows = slice(r * nq, (r + 1) * nq)
        ms.append((m_ref[0, :, sl] + jax.nn.sigmoid(gl_ref[rows, :]) * ylru[rows, :]).astype(BF16))
    for gi in range(nq // SUB):
        mg = jnp.concatenate([m[gi * SUB:(gi + 1) * SUB, :] for m in ms], axis=0)
        mn = _class_major(mg, perm_ref[...])
        dm = jnp.dot(mn, wo_ref[...], preferred_element_type=F32)
        xo = x_ref[0, gi * PERM:(gi + 1) * PERM, :] + dm
        o_ref[0, gi * PERM:(gi + 1) * PERM, :] = _rms(xo, gf_ref[...])


def _cx_mul(a, b):
    return a[0] * b[0] - a[1] * b[1], a[0] * b[1] + a[1] * b[0]


def _prep_s5(a_re, a_im, log_dt, b_re, b_im, c_re, c_im):
    hi = lax.Precision.HIGHEST
    g = S5_GROUPS
    are, aim = a_re.astype(F32), a_im.astype(F32)
    dt = jnp.exp(log_dt.astype(F32))[..., None]
    lre, lim = are * dt, aim * dt

    def apow(n):
        nn = n.astype(F32)[None, None, :, None]
        mag = jnp.exp(lre[:, :, None, :] * nn)
        ang = lim[:, :, None, :] * nn
        return mag * jnp.cos(ang), mag * jnp.sin(ang)

    ab = (jnp.exp(lre) * jnp.cos(lim), jnp.exp(lre) * jnp.sin(lim))
    den = are * are + aim * aim
    quo = (((ab[0] - 1.0) * are + ab[1] * aim) / den, (ab[1] * are - (ab[0] - 1.0) * aim) / den)
    bbar = _cx_mul((quo[0][..., None], quo[1][..., None]), (b_re.astype(F32), b_im.astype(F32)))
    cmat = (c_re.astype(F32), c_im.astype(F32))

    pw = apow(jnp.arange(SUB + 1))
    cp = _cx_mul((cmat[0][:, :, None], cmat[1][:, :, None]),
                 (pw[0][:, :, :SUB, None, :], pw[1][:, :, :SUB, None, :]))
    kern = (jnp.einsum('dgkip,dgpj->dgkij', cp[0], bbar[0], precision=hi)
            - jnp.einsum('dgkip,dgpj->dgkij', cp[1], bbar[1], precision=hi))
    kf, kb = kern[0], kern[1]
    by_lag = jnp.concatenate([kf[:, :0:-1], kf[:, :1] + kb[:, :1], kb[:, 1:]], axis=1)
    m = jnp.stack([by_lag[:, SUB - 1 - rp:2 * SUB - 1 - rp] for rp in range(SUB)], axis=1)
    mt = m.transpose(0, 1, 3, 2, 4).reshape(g, 256, 256)

    def cat(zf, zb, sign=1.0):
        return jnp.concatenate([zf[0], zb[0], sign * zf[1], sign * zb[1]], axis=-1)

    def sel(z, d, fn):
        return fn(z[0][d]), fn(z[1][d])

    bt = [(bbar[0][d].transpose(0, 2, 1)[:, None], bbar[1][d].transpose(0, 2, 1)[:, None])
          for d in range(2)]
    wf = _cx_mul(sel(pw, 0, lambda z: z[:, ::-1][:, 1:, None, :]), bt[0])
    wb = _cx_mul(sel(pw, 1, lambda z: z[:, :SUB, None, :]), bt[1])
    wbt = cat(wf, wb).reshape(g, 256, 256).transpose(0, 2, 1)
    cf = _cx_mul(sel(cmat, 0, lambda z: z[:, None]), sel(pw, 0, lambda z: z[:, 1:, None, :]))
    cb = _cx_mul(sel(cmat, 1, lambda z: z[:, None]),
                 sel(pw, 1, lambda z: z[:, ::-1][:, :SUB, None, :]))
    planes = jnp.stack([cf[0], -cf[1], cb[0], -cb[1]], axis=3)
    planes = planes.reshape(g // 2, 2, 256, 4, S5_STATE)
    wct = jnp.einsum('aerkp,ef->aerkfp', planes, jnp.eye(2, dtype=F32)).reshape(g // 2, 512, 512)

    def pair_lanes(z):
        rows = z.shape[1]
        return z.reshape(g // 2, 2, rows, S5_STATE).transpose(0, 2, 1, 3).reshape(g // 2, rows, 2 * S5_STATE)

    def tab(nf, nb, mf, mb, backward_only=False):
        zf = sel(apow(nf), 0, lambda z: pair_lanes(z * mf[None, :, None]))
        zb = sel(apow(nb), 1, lambda z: pair_lanes(z * mb[None, :, None]))
        return jnp.concatenate(([] if backward_only else [zf[0], zf[1]]) + [zb[0], zb[1]], axis=-1)

    m8 = jnp.arange(SUBLANES)
    nblk = Q_S5 // SUBLANES
    m16 = jnp.arange(nblk)
    parts = []
    for s in (1, 2, 4):
        n = jnp.full((SUBLANES,), SUB * s)
        parts.append(tab(n, n, (m8 >= s).astype(F32), (m8 < SUBLANES - s).astype(F32)))
    for s in (1, 2, 4, 8):
        n = jnp.full((nblk,), SUB * SUBLANES * s)
        parts.append(tab(n, n, (m16 >= s).astype(F32), (m16 < nblk - s).astype(F32)))
    one8 = jnp.ones((SUBLANES,), F32)
    parts.append(tab(SUB * (m8 + 1), SUB * (SUBLANES - m8), one8, one8))
    for n in (SUB, SUB * Q_S5):
        nn = jnp.full((SUBLANES,), n)
        parts.append(tab(nn, nn, one8, one8))
    table = jnp.concatenate(parts, axis=1)
    nq = SUB * jnp.arange(Q_S5)
    wq = tab(nq, nq, jnp.ones((Q_S5,), F32), jnp.ones((Q_S5,), F32), backward_only=True)
    return mt.astype(BF16), wbt.astype(BF16), wct.astype(BF16), table, wq


def _pair_heads(w):
    w = w.astype(BF16).reshape(2, LRU_HEADS // 2, 2, LRU_BLOCK, LRU_BLOCK)
    z = jnp.zeros_like(w[:, :, 0])
    top = jnp.concatenate([w[:, :, 0], z], axis=-1)
    bot = jnp.concatenate([z, w[:, :, 1]], axis=-1)
    return jnp.concatenate([top, bot], axis=-2)


def _full(shape):
    return pl.BlockSpec(shape, lambda *_: (0,) * len(shape))


def _params(sem, **kw):
    return pltpu.CompilerParams(dimension_semantics=sem, vmem_limit_bytes=VMEM_LIMIT, **kw)


def _trunk(x, w):
    bsz, seq, _ = x.shape
    lq = seq // SUB
    assert seq % (SUB * Q_S5) == 0
    row_shape = jax.ShapeDtypeStruct((bsz, lq, SUB * D_MODEL), F32)

    nt = lq // Q_S5
    uz = pl.pallas_call(
        _in_t_kernel,
        grid=(bsz, nt, 2),
        in_specs=[pl.BlockSpec((1, SUB * Q_S5, D_MODEL), lambda b, t, c: (b, t, 0)),
                  _full((1, D_MODEL)), _full((PERM, PERM)),
                  pl.BlockSpec((D_MODEL, D_MODEL), lambda b, t, c: (c, 0))],
        out_specs=pl.BlockSpec((1, SUB, D_MODEL, Q_S5), lambda b, t, c: (b, 0, c, t)),
        out_shape=jax.ShapeDtypeStruct((bsz, SUB, 2 * D_MODEL, lq), F32),
        scratch_shapes=[pltpu.VMEM((SUB * Q_S5, D_MODEL), BF16)],
        compiler_params=_params(("parallel", "parallel", "arbitrary")),
        name="in_proj_t",
    )(x, w["norm_g"], w["perm"], w["w_s5_t"])

    rows = pl.BlockSpec((1, Q_IN, SUB * D_MODEL), lambda b, t: (b, t, 0))
    g_s5 = pl.pallas_call(
        _in_n_kernel,
        grid=(bsz, lq // Q_IN),
        in_specs=[pl.BlockSpec((1, SUB * Q_IN, D_MODEL), lambda b, t: (b, t, 0)),
                  _full((1, D_MODEL)), _full((PERM, PERM)), _full((D_MODEL, D_MODEL))],
        out_specs=rows,
        out_shape=row_shape,
        compiler_params=_params(("parallel", "parallel")),
        name="in_proj_n",
    )(x, w["norm_g"], w["perm"], w["w_gs5"])

    def tile_of(ph, i):
        return jnp.where(ph == 0, nt - 1 - i, i)
    npair = GROUPS_PER_STEP // 2
    slab = GROUPS_PER_STEP * S5_GROUP_SIZE
    gw = pl.BlockSpec((GROUPS_PER_STEP, 256, 256), lambda g, b, ph, i: (g, 0, 0))
    y_t = pl.pallas_call(
        functools.partial(_s5_kernel, nt=nt),
        grid=(S5_GROUPS // GROUPS_PER_STEP, bsz, 2, nt),
        in_specs=[pl.BlockSpec((1, SUB, slab, Q_S5), lambda g, b, ph, i: (b, 0, g, tile_of(ph, i))),
                  pl.BlockSpec((slab, LANES), lambda g, b, ph, i: (g, 0)),
                  gw, gw,
                  pl.BlockSpec((npair, 512, 512), lambda g, b, ph, i: (g, 0, 0)),
                  pl.BlockSpec((npair, T_ROWS, 4 * LANES), lambda g, b, ph, i: (g, 0, 0)),
                  pl.BlockSpec((npair, Q_S5, 2 * LANES), lambda g, b, ph, i: (g, 0, 0))],
        out_specs=pl.BlockSpec((1, SUB, slab, Q_S5), lambda g, b, ph, i: (b, 0, g, ph * i)),
        out_shape=jax.ShapeDtypeStruct((bsz, SUB, D_MODEL, lq), F32),
        scratch_shapes=[pltpu.VMEM((npair, Q_S5, 4 * LANES), F32),
                        pltpu.VMEM((npair, SUBLANES, 2 * LANES), F32),
                        pltpu.VMEM((npair, SUBLANES, 2 * LANES), F32),
                        pltpu.VMEM((npair, SUBLANES, 2 * LANES), F32),
                        pltpu.VMEM((nt, npair, SUBLANES, 2 * LANES), F32)],
        compiler_params=_params(("arbitrary",) * 4),
        name="s5_ssm",
    )(uz, w["s5_d_b"], w["s5_mt"], w["s5_wbt"], w["s5_wct"], w["s5_tab"], w["s5_wq"])

    m_s5 = pl.pallas_call(
        _s5_post_kernel,
        grid=(bsz, nt, SUB // R_BLK),
        in_specs=[pl.BlockSpec((1, R_BLK, D_MODEL, Q_S5), lambda b, t, r: (b, r, 0, t)),
                  pl.BlockSpec((1, R_BLK, D_MODEL, Q_S5), lambda b, t, r: (b, r, 1, t)),
                  pl.BlockSpec((1, Q_S5, R_BLK * D_MODEL), lambda b, t, r: (b, t, r)),
                  _full((D_MODEL, D_MODEL)), _full((D_MODEL, LANES)), _full((D_MODEL, D_MODEL))],
        out_specs=pl.BlockSpec((1, Q_S5, R_BLK * D_MODEL), lambda b, t, r: (b, t, r)),
        out_shape=row_shape,
        compiler_params=_params(("parallel", "parallel", "parallel")),
        name="s5_post",
    )(y_t, uz, g_s5, w["glu_wt"], w["glu_b_b"], w["s5_proj"])

    ntb = lq // Q_LRU_B
    rows_b = pl.BlockSpec((1, Q_LRU_B, SUB * D_MODEL), lambda b, i: (b, ntb - 1 - i, 0))
    gate_w = _full((LRU_HEADS // 2, LRU_PAIR, LRU_PAIR))
    gate_b = _full((LRU_HEADS // 2, 1, LRU_PAIR))
    xc, h_bwd = pl.pallas_call(
        functools.partial(_lru_bwd_kernel, nt=ntb),
        grid=(bsz, ntb),
        in_specs=[pl.BlockSpec((1, SUB * Q_LRU_B, D_MODEL), lambda b, i: (b, ntb - 1 - i, 0)),
                  pl.BlockSpec((1, SUB, D_MODEL),
                               lambda b, i: (b, jnp.maximum((ntb - 1 - i) * Q_LRU_B - 1, 0), 0)),
                  _full((1, D_MODEL)), _full((PERM, PERM)), _full((D_MODEL, D_MODEL)),
                  _full((4, D_MODEL)), _full((1, D_MODEL)),
                  gate_w, gate_b, gate_w, gate_b, gate_b],
        out_specs=[rows_b, rows_b],
        out_shape=[row_shape, row_shape],
        scratch_shapes=[pltpu.VMEM((1, Q_LRU_B, SUB * D_MODEL), F32),
                        pltpu.VMEM((LRU_HEADS // 2, SUB, LRU_PAIR), F32),
                        pltpu.VMEM((SUB * Q_LRU_B, D_MODEL), BF16),
                        pltpu.VMEM((2, SUB, Q_LRU_B, LRU_PAIR), F32),
                        pltpu.VMEM((SUBLANES, D_MODEL), F32),
                        pltpu.VMEM((SUBLANES, D_MODEL), F32)],
        compiler_params=_params(("arbitrary", "arbitrary")),
        name="lru_bwd",
    )(x, x, w["norm_g"], w["perm"], w["w_ulru"], w["conv_w"], w["conv_b"],
      w["wa"][1], w["ba"][1], w["wx"][1], w["bx"][1], w["cvec"][1])

    rows_o = pl.BlockSpec((1, Q_OUT, SUB * D_MODEL), lambda b, i: (b, i, 0))
    x_o = pl.BlockSpec((1, SUB * Q_OUT, D_MODEL), lambda b, i: (b, i, 0))
    out = pl.pallas_call(
        _out_kernel,
        grid=(bsz, lq // Q_OUT),
        in_specs=[x_o] + [rows_o] * 3 + [_full((1, D_MODEL)), _full((D_MODEL, 2 * D_MODEL)),
                                         gate_w, gate_b, gate_w, gate_b, gate_b,
                                         _full((D_MODEL, D_MODEL)), _full((D_MODEL, D_MODEL)),
                                         _full((1, D_MODEL)), _full((PERM, PERM))],
        out_specs=x_o,
        out_shape=jax.ShapeDtypeStruct((bsz, seq, D_MODEL), F32),
        scratch_shapes=[pltpu.VMEM((1, Q_OUT, SUB * D_MODEL), F32),
                        pltpu.VMEM((2, SUB, Q_OUT, LRU_PAIR), F32),
                        pltpu.VMEM((SUBLANES, D_MODEL), F32),
                        pltpu.VMEM((SUB * Q_OUT, D_MODEL), BF16),
                        pltpu.VMEM((SUB * Q_OUT, D_MODEL), BF16),
                        pltpu.VMEM((SUB * Q_OUT, D_MODEL), F32),
                        pltpu.VMEM((SUB * Q_OUT, D_MODEL), F32)],
        compiler_params=_params(("arbitrary", "arbitrary")),
        name="lru_fwd_out",
    )(x, xc, h_bwd, m_s5, w["norm_g"], w["w_zg"], w["wa"][0], w["ba"][0], w["wx"][0], w["bx"][0],
      w["cvec"][0], w["lru_proj"], w["w_out"], w["norm_f_g"], w["perm"])
    return out


def kernel(x_prompt, x_sample, norm_g, w_in, s5_a_re, s5_a_im, s5_log_dt, s5_b_re, s5_b_im, s5_c_re, s5_c_im, s5_d, s5_glu_w, s5_glu_b, s5_proj, lru_conv_w, lru_conv_b, lru_lambda, lru_wa, lru_ba, lru_wx, lru_bx, lru_proj, w_out, norm_f_g):
    assert norm_g.shape[0] == 1, "single-layer trunk"
    wi = w_in[0]
    mt, wbt, wct, tab, wq = _prep_s5(s5_a_re[0], s5_a_im[0], s5_log_dt[0], s5_b_re[0], s5_b_im[0],
                                     s5_c_re[0], s5_c_im[0])
    tok = jnp.arange(PERM)
    w = {
        "perm": (tok[None, :] == SUB * (tok % SUB)[:, None] + (tok // SUB)[:, None]).astype(BF16),
        "norm_g": norm_g[0].reshape(1, D_MODEL).astype(F32),
        "w_s5_t": wi[:, :2 * D_MODEL].T.astype(BF16),
        "w_ulru": wi[:, 2 * D_MODEL:3 * D_MODEL].astype(BF16),
        "w_gs5": wi[:, 4 * D_MODEL:5 * D_MODEL].astype(BF16),
        "w_zg": jnp.concatenate([wi[:, 3 * D_MODEL:4 * D_MODEL], wi[:, 5 * D_MODEL:]], axis=1).astype(BF16),
        "s5_mt": mt, "s5_wbt": wbt, "s5_wct": wct, "s5_tab": tab, "s5_wq": wq,
        "s5_d_b": jnp.broadcast_to(s5_d[0].astype(F32)[:, None], (D_MODEL, LANES)),
        "glu_wt": s5_glu_w[0].T.astype(BF16),
        "glu_b_b": jnp.broadcast_to(s5_glu_b[0].astype(F32)[:, None], (D_MODEL, LANES)),
        "s5_proj": s5_proj[0].astype(BF16),
        "conv_w": lru_conv_w[0].astype(F32),
        "conv_b": lru_conv_b[0].reshape(1, D_MODEL).astype(F32),
        "wa": _pair_heads(lru_wa[0]),
        "wx": _pair_heads(lru_wx[0]),
        "ba": lru_ba[0].reshape(2, LRU_HEADS // 2, 1, LRU_PAIR).astype(F32),
        "bx": lru_bx[0].reshape(2, LRU_HEADS // 2, 1, LRU_PAIR).astype(F32),
        "cvec": (-RG_C * jax.nn.softplus(-lru_lambda[0].astype(F32))).reshape(2, LRU_HEADS // 2, 1, LRU_PAIR),
        "lru_proj": lru_proj[0].astype(BF16),
        "w_out": w_out[0].astype(BF16),
        "norm_f_g": norm_f_g.reshape(1, D_MODEL).astype(F32),
    }
    return (_trunk(x_prompt, w), _trunk(x_sample, w))
```
